```python
import jax, jax.numpy as jnp
from jax import lax
import numpy as np

D_MODEL = 1024
BATCH = 8
SEQ = 4096
DEPTH = 1

DN_HEADS = 8
DN_HEAD_DIM = 128
DN_WIDTH = DN_HEADS * DN_HEAD_DIM
DN_CONV = 4
DN_CHUNK = 64
SB_HEADS = 8
SB_HEAD_DIM = 128
SB_WIDTH = SB_HEADS * SB_HEAD_DIM
SB_BLOCK = 128
D_FF = 2816
FFN_CONV = 3
EPS = 1e-6

DN_QKV_END = 3 * DN_WIDTH
DN_A_END = DN_QKV_END + DN_HEADS
DN_B_END = DN_A_END + DN_HEADS
DN_G_END = DN_B_END + DN_WIDTH
SB_QKV_END = DN_G_END + 3 * SB_WIDTH
IN_WIDTH = SB_QKV_END + 2 * D_MODEL
SPLIT_IDX = (DN_QKV_END, DN_A_END, DN_B_END, DN_G_END, SB_QKV_END)

kernel_name = 'hybrid_gdn_stickbreak_convffn'


def rmsnorm(x, w):
    xf = x.astype(jnp.float32)
    y = xf * lax.rsqrt(jnp.mean(xf * xf, axis=-1, keepdims=True) + EPS)
    return (y * w.astype(jnp.float32)).astype(x.dtype)


def l2norm(x):
    xf = x.astype(jnp.float32)
    return (xf * lax.rsqrt(jnp.sum(xf * xf, axis=-1, keepdims=True) + EPS)).astype(x.dtype)


def causal_dwconv(x, w):
    K = w.shape[0]
    T = x.shape[1]
    xp = jnp.pad(x, ((0, 0), (K - 1, 0), (0, 0)))
    y = w[0] * xp[:, 0:T]
    for i in range(1, K):
        y = y + w[i] * xp[:, i:i + T]
    return y


def to_heads(t, n_heads, head_dim):
    B, T, _ = t.shape
    return t.reshape(B, T, n_heads, head_dim).transpose(0, 2, 1, 3)


def gated_delta_rule(q, k, v, g, beta):
    out_dtype = v.dtype
    q, k, v, g, beta = (t.astype(jnp.float32) for t in (q, k, v, g, beta))
    B, H, T, dk = q.shape
    dv = v.shape[-1]
    C = DN_CHUNK
    N = T // C
    q = q * (dk ** -0.5)
    rs = lambda t: t.reshape(B, H, N, C, *t.shape[3:])
    q, k, v, g, beta = rs(q), rs(k), rs(v), rs(g), rs(beta)
    g = jnp.cumsum(g, axis=-1)
    k_beta = k * beta[..., None]
    v_beta = v * beta[..., None]
    lower = jnp.tril(jnp.ones((C, C), dtype=bool))
    strict = jnp.tril(jnp.ones((C, C), dtype=bool), -1)
    diff = g[..., :, None] - g[..., None, :]
    decay = jnp.where(lower, jnp.exp(jnp.where(lower, diff, 0.0)), 0.0)
    L = jnp.where(strict, jnp.einsum('bhncd,bhnsd->bhncs', k_beta, k) * decay, 0.0)
    rhs = jnp.concatenate([v_beta, k_beta * jnp.exp(g)[..., None]], axis=-1)
    sol = lax.linalg.triangular_solve(L, rhs, left_side=True, lower=True, unit_diagonal=True)
    u = sol[..., :dv]
    w = sol[..., dv:]
    a_qk = jnp.where(lower, jnp.einsum('bhncd,bhnsd->bhncs', q, k) * decay, 0.0)

    def step(S, xs):
        q_i, k_i, u_i, w_i, g_i, a_i = xs
        v_new = u_i - jnp.einsum('bhcd,bhde->bhce', w_i, S)
        o_i = (jnp.einsum('bhcd,bhde->bhce', q_i * jnp.exp(g_i)[..., None], S)
               + jnp.einsum('bhcs,bhse->bhce', a_i, v_new))
        g_last = g_i[..., -1]
        S = (S * jnp.exp(g_last)[..., None, None]
             + jnp.einsum('bhcd,bhce->bhde', k_i * jnp.exp(g_last[..., None] - g_i)[..., None], v_new))
        return S, o_i

    xs = tuple(jnp.moveaxis(t, 2, 0) for t in (q, k, u, w, g, a_qk))
    S0 = jnp.zeros((B, H, dk, dv), jnp.float32)
    _, o = lax.scan(step, S0, xs)
    o = jnp.moveaxis(o, 0, 2).reshape(B, H, T, dv)
    return o.astype(out_dtype)


def stick_breaking_attention(q, k, v):
    B, H, T, d = q.shape
    nb = T // SB_BLOCK
    scale = d ** -0.5
    qb = q.reshape(B, H, nb, SB_BLOCK, d).transpose(2, 0, 1, 3, 4)
    key_pos = jnp.arange(T)

    def block(args):
        q_blk, i = args
        z = jnp.einsum('bhqd,bhkd->bhqk', q_blk, k).astype(jnp.float32) * scale
        q_pos = i * SB_BLOCK + jnp.arange(SB_BLOCK)
        mask = key_pos[None, :] < q_pos[:, None]
        log_keep = jnp.where(mask, jax.nn.log_sigmoid(-z), 0.0)
        between = lax.cumsum(log_keep, axis=3, reverse=True) - log_keep
        log_a = jax.nn.log_sigmoid(z) + between
        a = jnp.where(mask, jnp.exp(log_a), 0.0)
        return jnp.einsum('bhqk,bhkd->bhqd', a.astype(v.dtype), v)

    o = lax.map(block, (qb, jnp.arange(nb)))
    return o.transpose(1, 2, 0, 3, 4).reshape(B, H, T, d)


def setup_inputs(seed: int = 0) -> dict:
    key = jax.random.key(seed)
    ks = jax.random.split(key, 20)
    L = DEPTH
    nrm = lambda k, shape, fan_in: jax.random.normal(k, shape, jnp.float32) * (fan_in ** -0.5)
    gain = lambda k, shape: 1.0 + 0.02 * jax.random.normal(k, shape, jnp.float32)
    x = jax.random.normal(ks[0], (BATCH, SEQ, D_MODEL), jnp.float32)
    norm1_w = gain(ks[1], (L, D_MODEL))
    w_in = nrm(ks[2], (L, D_MODEL, IN_WIDTH), D_MODEL)
    dn_conv_w = nrm(ks[3], (L, DN_CONV, 3 * DN_WIDTH), DN_CONV)
    dn_A_log = jnp.log(jax.random.uniform(ks[4], (L, DN_HEADS), jnp.float32, 1.0, 16.0))
    dt = jnp.exp(jax.random.uniform(ks[5], (L, DN_HEADS), jnp.float32, np.log(1e-3), np.log(1e-1)))
    dn_dt_bias = dt + jnp.log(-jnp.expm1(-dt))
    dn_norm_w = gain(ks[6], (L, DN_HEAD_DIM))
    w_proj_dn = nrm(ks[7], (L, DN_WIDTH, D_MODEL), DN_WIDTH)
    w_proj_sb = nrm(ks[8], (L, SB_WIDTH, D_MODEL), SB_WIDTH)
    w_out = nrm(ks[9], (L, D_MODEL, D_MODEL), D_MODEL)
    norm2_w = gain(ks[10], (L, D_MODEL))
    ffn_w_up = nrm(ks[11], (L, D_MODEL, 2 * D_FF), D_MODEL)
    ffn_conv_w = nrm(ks[12], (L, FFN_CONV, 2 * D_FF), FFN_CONV)
    ffn_w_down = nrm(ks[13], (L, D_FF, D_MODEL), D_FF)
    norm_f_w = gain(ks[14], (D_MODEL,))
    return {'x': x, 'norm1_w': norm1_w, 'w_in': w_in, 'dn_conv_w': dn_conv_w,
            'dn_A_log': dn_A_log, 'dn_dt_bias': dn_dt_bias, 'dn_norm_w': dn_norm_w,
            'w_proj_dn': w_proj_dn, 'w_proj_sb': w_proj_sb, 'w_out': w_out,
            'norm2_w': norm2_w, 'ffn_w_up': ffn_w_up, 'ffn_conv_w': ffn_conv_w,
            'ffn_w_down': ffn_w_down, 'norm_f_w': norm_f_w}


def reference(x, norm1_w, w_in, dn_conv_w, dn_A_log, dn_dt_bias, dn_norm_w,
              w_proj_dn, w_proj_sb, w_out, norm2_w, ffn_w_up, ffn_conv_w,
              ffn_w_down, norm_f_w):
    B, T, _ = x.shape
    for l in range(DEPTH):
        n1 = rmsnorm(x, norm1_w[l])
        h = n1 @ w_in[l]
        dn_qkv, dn_a, dn_b, dn_gate, sb_qkv, gate_logits = jnp.split(h, SPLIT_IDX, axis=-1)

        dn_qkv = jax.nn.silu(causal_dwconv(dn_qkv, dn_conv_w[l]))
        dq, dk, dv = jnp.split(dn_qkv, 3, axis=-1)
        dq = l2norm(to_heads(dq, DN_HEADS, DN_HEAD_DIM))
        dk = l2norm(to_heads(dk, DN_HEADS, DN_HEAD_DIM))
        dv = to_heads(dv, DN_HEADS, DN_HEAD_DIM)
        beta = jax.nn.sigmoid(dn_b.astype(jnp.float32)).transpose(0, 2, 1)
        g = (-jnp.exp(dn_A_log[l].astype(jnp.float32))
             * jax.nn.softplus(dn_a.astype(jnp.float32) + dn_dt_bias[l].astype(jnp.float32))).transpose(0, 2, 1)
        o_dn = gated_delta_rule(dq, dk, dv, g, beta)
        o_dn = rmsnorm(o_dn, dn_norm_w[l]).transpose(0, 2, 1, 3)
        o_dn = o_dn * jax.nn.silu(dn_gate.reshape(B, T, DN_HEADS, DN_HEAD_DIM))
        o_dn = o_dn.reshape(B, T, DN_WIDTH)

        sq, sk, sv = jnp.split(sb_qkv, 3, axis=-1)
        o_sb = stick_breaking_attention(to_heads(sq, SB_HEADS, SB_HEAD_DIM),
                                        to_heads(sk, SB_HEADS, SB_HEAD_DIM),
                                        to_heads(sv, SB_HEADS, SB_HEAD_DIM))
        o_sb = o_sb.transpose(0, 2, 1, 3).reshape(B, T, SB_WIDTH)

        gate_dn, gate_sb = jnp.split(jax.nn.sigmoid(gate_logits), 2, axis=-1)
        mixed = gate_dn * (o_dn @ w_proj_dn[l]) + gate_sb * (o_sb @ w_proj_sb[l])
        x = x + mixed @ w_out[l]

        n2 = rmsnorm(x, norm2_w[l])
        u = causal_dwconv(n2 @ ffn_w_up[l], ffn_conv_w[l])
        gate, up = jnp.split(u, 2, axis=-1)
        x = x + (jax.nn.silu(gate) * up) @ ffn_w_down[l]
    return rmsnorm(x, norm_f_w)
```

```python
import functools

import jax
import jax.numpy as jnp
from jax import lax
from jax.experimental import pallas as pl
from jax.experimental.pallas import tpu as pltpu

F32 = jnp.float32
BF16 = jnp.bfloat16

D_MODEL = 1024
HEADS = 8
HEAD_DIM = 128
WIDTH = HEADS * HEAD_DIM
DN_CONV = 4
CHUNK = 64
D_FF = 2816
FFN_CONV = 3
EPS = 1e-6

H_COLS = 3 * WIDTH + WIDTH + 3 * WIDTH + 2 * D_MODEL
COL_DN_GATE = 3 * WIDTH // 128
COL_SB_Q = COL_DN_GATE + WIDTH // 128
COL_SB_K = COL_SB_Q + HEADS
COL_SB_V = COL_SB_K + HEADS
COL_MERGE = COL_SB_V + HEADS
AB_LANES = 128
HALO = 8

VMEM_LIMIT = 56 * 1024 * 1024

HIGHEST = lax.Precision.HIGHEST


def _dot(a, b, precision=None):
    return jnp.dot(a, b, preferred_element_type=F32, precision=precision)


def _dot_nt(a, b):
    return lax.dot_general(a, b, (((1,), (1,)), ((), ())), preferred_element_type=F32)


def _dot_tn(a, b):
    return lax.dot_general(a, b, (((0,), (0,)), ((), ())), preferred_element_type=F32)


def _sigmoid(x):
    return 1.0 / (1.0 + jnp.exp(-x))


def _softplus(x):
    return jnp.maximum(x, 0.0) + jnp.log(1.0 + jnp.exp(-jnp.abs(x)))


def _in_proj_kernel(x_ref, nw_ref, w_ref, wab_ref, h_ref, ab_ref, xn_ref):
    @pl.when(pl.program_id(1) == 0)
    def _():
        x = x_ref[...]
        ms = jnp.mean(x * x, axis=-1, keepdims=True)
        xn_ref[...] = (x * lax.rsqrt(ms + EPS) * nw_ref[...]).astype(BF16)
        ab_ref[...] = _dot(xn_ref[...], wab_ref[...])

    h_ref[...] = _dot(xn_ref[...], w_ref[...]).astype(h_ref.dtype)


def _in_proj(x2, norm_w, w_main, w_ab, tm, tn):
    m = x2.shape[0]
    n = w_main.shape[1]
    return pl.pallas_call(
        _in_proj_kernel,
        name="in_proj",
        grid=(m // tm, n // tn),
        in_specs=[
            pl.BlockSpec((tm, D_MODEL), lambda i, j: (i, 0)),
            pl.BlockSpec((1, D_MODEL), lambda i, j: (0, 0)),
            pl.BlockSpec((D_MODEL, tn), lambda i, j: (0, j)),
            pl.BlockSpec((D_MODEL, AB_LANES), lambda i, j: (0, 0)),
        ],
        out_specs=[
            pl.BlockSpec((tm, tn), lambda i, j: (i, j)),
            pl.BlockSpec((tm, AB_LANES), lambda i, j: (i, 0)),
        ],
        out_shape=[
            jax.ShapeDtypeStruct((m, n), BF16),
            jax.ShapeDtypeStruct((m, AB_LANES), F32),
        ],
        scratch_shapes=[pltpu.VMEM((tm, D_MODEL), BF16)],
        compiler_params=pltpu.CompilerParams(
            dimension_semantics=("parallel", "arbitrary"),
            vmem_limit_bytes=VMEM_LIMIT),
    )(x2, norm_w, w_main, w_ab)


def _unit_lower_inverse(l_strict, row, col):
    eye = jnp.where(row == col, 1.0, 0.0).astype(F32)
    same16 = (row // 16) == (col // 16)
    same32 = (row // 32) == (col // 32)
    ld = jnp.where(same16, l_strict, 0.0)
    p = eye - ld
    l2 = _dot(ld, ld, HIGHEST)
    p = p + _dot(p, l2, HIGHEST)
    l4 = _dot(l2, l2, HIGHEST)
    p = p + _dot(p, l4, HIGHEST)
    l8 = _dot(l4, l4, HIGHEST)
    p = p + _dot(p, l8, HIGHEST)
    c1 = jnp.where(same32 & jnp.logical_not(same16), l_strict, 0.0)
    p = p - _dot(_dot(p, c1, HIGHEST), p, HIGHEST)
    c2 = jnp.where(same32, 0.0, l_strict)
    p = p - _dot(_dot(p, c2, HIGHEST), p, HIGHEST)
    return p


def _gdn_kernel(qkv_ref, cw_ref, ab_ref, abt_ref, alog_row_ref, dtb_row_ref,
                alog_col_ref, dtb_col_ref, gate_ref, nw_ref, o_ref,
                s_ref, halo_ref, ext_ref, act_ref, gcol_ref, bcol_ref, grow_ref, *, tc):
    @pl.when(pl.program_id(1) == 0)
    def _():
        s_ref[...] = jnp.zeros_like(s_ref)
        halo_ref[...] = jnp.zeros_like(halo_ref)

    x = qkv_ref[...].astype(F32)
    ext_ref[0:HALO, :] = halo_ref[...]
    ext_ref[HALO:HALO + tc, :] = x
    halo_ref[...] = x[tc - HALO:, :]
    cw = cw_ref[...]
    y = cw[3:4, :] * x
    for tap in range(DN_CONV - 1):
        shift = DN_CONV - 1 - tap
        y = y + cw[tap:tap + 1, :] * ext_ref[HALO - shift:HALO - shift + tc, :]
    act_ref[...] = y * _sigmoid(y)

    ab = ab_ref[...]
    gcol_ref[...] = -jnp.exp(alog_row_ref[...]) * _softplus(ab + dtb_row_ref[...])
    bcol_ref[...] = _sigmoid(ab)
    abt = abt_ref[...]
    grow_ref[...] = -jnp.exp(alog_col_ref[...]) * _softplus(abt[:, :HEADS, :] + dtb_col_ref[...])

    row = lax.broadcasted_iota(jnp.int32, (CHUNK, CHUNK), 0)
    col = lax.broadcasted_iota(jnp.int32, (CHUNK, CHUNK), 1)
    lower = col <= row
    strict = col < row
    nw = nw_ref[...]

    def chunk_body(i, carry):
        r0 = pl.multiple_of(i * CHUNK, CHUNK)
        rows = pl.ds(r0, CHUNK)
        g_cols = gcol_ref[rows, :]
        b_cols = bcol_ref[rows, :]
        g_rows = grow_ref[i]
        for h in range(HEADS):
            lanes = slice(h * HEAD_DIM, (h + 1) * HEAD_DIM)
            q = act_ref[rows, h * HEAD_DIM:(h + 1) * HEAD_DIM]
            k = act_ref[rows, WIDTH + h * HEAD_DIM:WIDTH + (h + 1) * HEAD_DIM]
            v = act_ref[rows, 2 * WIDTH + h * HEAD_DIM:2 * WIDTH + (h + 1) * HEAD_DIM]
            q = q * (lax.rsqrt(jnp.sum(q * q, axis=-1, keepdims=True) + EPS) * (HEAD_DIM ** -0.5))
            k = k * lax.rsqrt(jnp.sum(k * k, axis=-1, keepdims=True) + EPS)
            g_c = g_cols[:, h:h + 1]
            g_r = g_rows[h:h + 1, :]
            beta = b_cols[:, HEADS + h:HEADS + h + 1]
            gc_c = jnp.sum(jnp.where(lower, g_r, 0.0), axis=1, keepdims=True)
            gc_r = jnp.sum(jnp.where(row <= col, g_c, 0.0), axis=0, keepdims=True)
            decay = jnp.where(lower, jnp.exp(jnp.where(lower, gc_c - gc_r, 0.0)), 0.0)
            kb = k * beta
            vb = v * beta
            kk_qk = _dot_nt(jnp.concatenate([kb, q], axis=0).astype(BF16), k.astype(BF16))
            l_strict = jnp.where(strict, kk_qk[:CHUNK] * decay, 0.0)
            a_qk = kk_qk[CHUNK:] * decay
            t_inv = _unit_lower_inverse(l_strict, row, col)
            eg = jnp.exp(gc_c)
            rhs = jnp.concatenate([vb, kb * eg], axis=1)
            uw = _dot(t_inv, rhs, HIGHEST)
            u = uw[:, :HEAD_DIM]
            w = uw[:, HEAD_DIM:]
            s = s_ref[h]
            ws_qs = _dot(jnp.concatenate([w, q * eg], axis=0).astype(BF16), s.astype(BF16))
            v_new = u - ws_qs[:CHUNK]
            o = ws_qs[CHUNK:] + _dot(a_qk.astype(BF16), v_new.astype(BF16))
            g_last = gc_c[CHUNK - 1:CHUNK, :]
            kd = k * jnp.exp(g_last - gc_c)
            s_ref[h] = s * jnp.exp(g_last) + _dot_tn(kd.astype(BF16), v_new.astype(BF16))
            o = o * lax.rsqrt(jnp.mean(o * o, axis=-1, keepdims=True) + EPS) * nw
            gate = gate_ref[rows, lanes].astype(F32)
            o_ref[rows, lanes] = (o * (gate * _sigmoid(gate))).astype(o_ref.dtype)
        return carry

    lax.fori_loop(0, tc // CHUNK, chunk_body, 0)


def _gdn(h3, conv_w, ab3, abt, alog, dtb, norm_w, tc):
    b, t, _ = h3.shape
    nc = tc // CHUNK
    alog_row = jnp.zeros((1, AB_LANES), F32).at[0, :HEADS].set(alog)
    dtb_row = jnp.zeros((1, AB_LANES), F32).at[0, :HEADS].set(dtb)
    kern = functools.partial(_gdn_kernel, tc=tc)
    return pl.pallas_call(
        kern,
        name="gdn",
        grid=(b, t // tc),
        in_specs=[
            pl.BlockSpec((None, tc, 3 * WIDTH), lambda bi, ti: (bi, ti, 0)),
            pl.BlockSpec((DN_CONV, 3 * WIDTH), lambda bi, ti: (0, 0)),
            pl.BlockSpec((None, tc, AB_LANES), lambda bi, ti: (bi, ti, 0)),
            pl.BlockSpec((None, nc, 2 * HEADS, CHUNK), lambda bi, ti: (bi, ti, 0, 0)),
            pl.BlockSpec((1, AB_LANES), lambda bi, ti: (0, 0)),
            pl.BlockSpec((1, AB_LANES), lambda bi, ti: (0, 0)),
            pl.BlockSpec((HEADS, 1), lambda bi, ti: (0, 0)),
            pl.BlockSpec((HEADS, 1), lambda bi, ti: (0, 0)),
            pl.BlockSpec((None, tc, WIDTH), lambda bi, ti: (bi, ti, COL_DN_GATE * 128 // WIDTH)),
            pl.BlockSpec((1, HEAD_DIM), lambda bi, ti: (0, 0)),
        ],
        out_specs=pl.BlockSpec((None, tc, WIDTH), lambda bi, ti: (bi, ti, 0)),
        out_shape=jax.ShapeDtypeStruct((b, t, WIDTH), BF16),
        scratch_shapes=[
            pltpu.VMEM((HEADS, HEAD_DIM, HEAD_DIM), F32),
            pltpu.VMEM((HALO, 3 * WIDTH), F32),
            pltpu.VMEM((HALO + tc, 3 * WIDTH), F32),
            pltpu.VMEM((tc, 3 * WIDTH), F32),
            pltpu.VMEM((tc, AB_LANES), F32),
            pltpu.VMEM((tc, AB_LANES), F32),
            pltpu.VMEM((nc, HEADS, CHUNK), F32),
        ],
        compiler_params=pltpu.CompilerParams(
            dimension_semantics=("parallel", "arbitrary"),
            vmem_limit_bytes=VMEM_LIMIT),
    )(h3, conv_w, ab3, abt, alog_row, dtb_row, alog.reshape(HEADS, 1), dtb.reshape(HEADS, 1),
      h3, norm_w.reshape(1, HEAD_DIM))


def _sb_kernel(q_ref, k_ref, v_ref, o_ref, *, tq):
    i = pl.program_id(2)
    q = q_ref[...]
    scale = HEAD_DIM ** -0.5
    row = lax.broadcasted_iota(jnp.int32, (tq, tq), 0)
    col = lax.broadcasted_iota(jnp.int32, (tq, tq), 1)
    later = jnp.where(row > col, 1.0, 0.0).astype(BF16)

    def block(j, acc, run, mask):
        k0 = pl.multiple_of(j * tq, tq)
        k = k_ref[pl.ds(k0, tq), :]
        v = v_ref[pl.ds(k0, tq), :]
        z = _dot_nt(q, k) * scale
        log_keep = -_softplus(z)
        if mask is not None:
            log_keep = jnp.where(mask, log_keep, 0.0)
        between = _dot(log_keep.astype(BF16), later) + run
        a = jnp.exp(z + log_keep + between)
        if mask is not None:
            a = jnp.where(mask, a, 0.0)
        acc = acc + _dot(a.astype(BF16), v)
        run = run + jnp.sum(log_keep, axis=1, keepdims=True)
        return acc, run

    acc0 = jnp.zeros((tq, HEAD_DIM), F32)
    run0 = jnp.zeros((tq, 1), F32)
    acc, run = block(i, acc0, run0, col < row)

    def body(jj, carry):
        return block(i - jj, carry[0], carry[1], None)

    acc, run = lax.fori_loop(1, i + 1, body, (acc, run))
    o_ref[...] = acc.astype(o_ref.dtype)


def _sb_attention(h3, tq):
    b, t, _ = h3.shape
    kern = functools.partial(_sb_kernel, tq=tq)
    return pl.pallas_call(
        kern,
        name="sb_attn",
        grid=(b, HEADS, t // tq),
        in_specs=[
            pl.BlockSpec((None, tq, HEAD_DIM), lambda bi, hi, qi: (bi, qi, COL_SB_Q + hi)),
            pl.BlockSpec((None, t, HEAD_DIM), lambda bi, hi, qi: (bi, 0, COL_SB_K + hi)),
            pl.BlockSpec((None, t, HEAD_DIM), lambda bi, hi, qi: (bi, 0, COL_SB_V + hi)),
        ],
        out_specs=pl.BlockSpec((None, tq, HEAD_DIM), lambda bi, hi, qi: (bi, qi, hi)),
        out_shape=jax.ShapeDtypeStruct((b, t, WIDTH), BF16),
        compiler_params=pltpu.CompilerParams(
            dimension_semantics=("parallel", "parallel", "arbitrary"),
            vmem_limit_bytes=VMEM_LIMIT),
    )(h3, h3, h3)


def _merge_kernel(x_ref, odn_ref, osb_ref, gdn_ref, gsb_ref, pa_ref, pb_ref, wo_ref, nw_ref,
                  x1_ref, n2_ref):
    pa = _dot(odn_ref[...], pa_ref[...])
    pb = _dot(osb_ref[...], pb_ref[...])
    mixed = (_sigmoid(gdn_ref[...].astype(F32)) * pa + _sigmoid(gsb_ref[...].astype(F32)) * pb)
    x1 = x_ref[...] + _dot(mixed.astype(BF16), wo_ref[...])
    x1_ref[...] = x1
    ms = jnp.mean(x1 * x1, axis=-1, keepdims=True)
    n2_ref[...] = (x1 * lax.rsqrt(ms + EPS) * nw_ref[...]).astype(n2_ref.dtype)


def _merge(x2, odn2, osb2, h2, p_dn, p_sb, w_out, norm_w, tm):
    m = x2.shape[0]
    gcol = COL_MERGE * 128 // D_MODEL
    row_spec = lambda c: pl.BlockSpec((tm, D_MODEL), lambda i: (i, c))
    w_spec = pl.BlockSpec((D_MODEL, D_MODEL), lambda i: (0, 0))
    return pl.pallas_call(
        _merge_kernel,
        name="merge",
        grid=(m // tm,),
        in_specs=[row_spec(0), row_spec(0), row_spec(0), row_spec(gcol), row_spec(gcol + 1),
                  w_spec, w_spec, w_spec, pl.BlockSpec((1, D_MODEL), lambda i: (0, 0))],
        out_specs=[row_spec(0), row_spec(0)],
        out_shape=[jax.ShapeDtypeStruct((m, D_MODEL), F32),
                   jax.ShapeDtypeStruct((m, D_MODEL), BF16)],
        compiler_params=pltpu.CompilerParams(
            dimension_semantics=("parallel",),
            vmem_limit_bytes=VMEM_LIMIT),
    )(x2, odn2, osb2, h2, h2, p_dn, p_sb, w_out, norm_w)


def _ffn_kernel(n2_ref, x1_ref, wg_ref, wu_ref, cg_ref, cu_ref, wd_ref, nw_ref, o_ref,
                acc_ref, ext_ref, carry_ref, *, tm):
    ti = pl.program_id(1)
    ki = pl.program_id(2)
    n2 = n2_ref[...]

    def conv(u, cw, slot):
        prev = carry_ref[ki, slot]
        ext_ref[0:HALO, :] = jnp.where(ti == 0, 0.0, prev)
        ext_ref[HALO:HALO + tm, :] = u
        carry_ref[ki, slot] = u[tm - HALO:, :]
        y = cw[2:3, :] * u
        y = y + cw[1:2, :] * ext_ref[HALO - 1:HALO - 1 + tm, :]
        y = y + cw[0:1, :] * ext_ref[HALO - 2:HALO - 2 + tm, :]
        return y

    gate = conv(_dot(n2, wg_ref[...]), cg_ref[...], 0)
    up = conv(_dot(n2, wu_ref[...]), cu_ref[...], 1)
    act = (gate * _sigmoid(gate) * up).astype(BF16)
    part = _dot(act, wd_ref[...])

    @pl.when(ki == 0)
    def _():
        acc_ref[...] = x1_ref[...] + part

    @pl.when(ki == pl.num_programs(2) - 1)
    def _():
        x2 = acc_ref[...] + part
        ms = jnp.mean(x2 * x2, axis=-1, keepdims=True)
        o_ref[...] = x2 * lax.rsqrt(ms + EPS) * nw_ref[...]


def _ffn(n2, x1, w_up, conv_w, w_down, norm_w, tm, tf):
    b, t, _ = n2.shape
    nk = D_FF // tf
    assert nk == 2
    kern = functools.partial(_ffn_kernel, tm=tm)
    return pl.pallas_call(
        kern,
        name="ffn",
        grid=(b, t // tm, nk),
        in_specs=[
            pl.BlockSpec((None, tm, D_MODEL), lambda bi, ti, ki: (bi, ti, 0)),
            pl.BlockSpec((None, tm, D_MODEL), lambda bi, ti, ki: (bi, ti, 0)),
            pl.BlockSpec((D_MODEL, tf), lambda bi, ti, ki: (0, ki)),
            pl.BlockSpec((D_MODEL, tf), lambda bi, ti, ki: (0, nk + ki)),
            pl.BlockSpec((FFN_CONV, tf), lambda bi, ti, ki: (0, ki)),
            pl.BlockSpec((FFN_CONV, tf), lambda bi, ti, ki: (0, nk + ki)),
            pl.BlockSpec((tf, D_MODEL), lambda bi, ti, ki: (ki, 0)),
            pl.BlockSpec((1, D_MODEL), lambda bi, ti, ki: (0, 0)),
        ],
        out_specs=pl.BlockSpec((None, tm, D_MODEL), lambda bi, ti, ki: (bi, ti, 0)),
        out_shape=jax.ShapeDtypeStruct((b, t, D_MODEL), F32),
        scratch_shapes=[
            pltpu.VMEM((tm, D_MODEL), F32),
            pltpu.VMEM((HALO + tm, tf), F32),
            pltpu.VMEM((nk, 2, HALO, tf), F32),
        ],
        compiler_params=pltpu.CompilerParams(
            dimension_semantics=("parallel", "arbitrary", "arbitrary"),
            vmem_limit_bytes=VMEM_LIMIT),
    )(n2, x1, w_up, w_up, conv_w, conv_w, w_down, norm_w)


def kernel(x, norm1_w, w_in, dn_conv_w, dn_A_log, dn_dt_bias, dn_norm_w, w_proj_dn, w_proj_sb,
           w_out, norm2_w, ffn_w_up, ffn_conv_w, ffn_w_down, norm_f_w):
    b, t, d = x.shape
    depth = w_in.shape[0]
    m = b * t
    tm = min(1024, t)
    tc = min(256, t)
    tq = min(256, t)
    qkv_end = 3 * WIDTH
    a_end = qkv_end + HEADS
    b_end = a_end + HEADS
    g_end = b_end + WIDTH
    for l in range(depth):
        wl = w_in[l]
        w_main = jnp.concatenate([wl[:, :qkv_end], wl[:, b_end:]], axis=1).astype(BF16)
        w_ab = jnp.pad(wl[:, qkv_end:b_end], ((0, 0), (0, AB_LANES - 2 * HEADS))).astype(BF16)

        h2, ab2 = _in_proj(x.reshape(m, d), norm1_w[l].reshape(1, d), w_main, w_ab, tm, 1024)
        h3 = h2.reshape(b, t, H_COLS)
        ab3 = ab2.reshape(b, t, AB_LANES)
        abt = ab3[:, :, :2 * HEADS].reshape(b, t // CHUNK, CHUNK, 2 * HEADS).transpose(0, 1, 3, 2)

        o_dn = _gdn(h3, dn_conv_w[l], ab3, abt, dn_A_log[l], dn_dt_bias[l], dn_norm_w[l], tc)
        o_sb = _sb_attention(h3, tq)

        x1, n2 = _merge(x.reshape(m, d), o_dn.reshape(m, WIDTH), o_sb.reshape(m, WIDTH), h2,
                        w_proj_dn[l].astype(BF16), w_proj_sb[l].astype(BF16),
                        w_out[l].astype(BF16), norm2_w[l].reshape(1, d), min(512, t))

        last = l == depth - 1
        nf = norm_f_w.reshape(1, d) if last else None
        assert last, "single-layer block"
        x = _ffn(n2.reshape(b, t, d), x1.reshape(b, t, d), ffn_w_up[l].astype(BF16),
                 ffn_conv_w[l], ffn_w_down[l].astype(BF16), nf, min(512, t), D_FF // 2)
    return x
```

```python
import functools
import math

import jax
import jax.numpy as jnp
from jax import lax
from jax.experimental import pallas as pl
from jax.experimental.pallas import tpu as pltpu

F32 = jnp.float32
BF16 = jnp.bfloat16

D_MODEL = 1024
HEADS = 8
HEAD_DIM = 128
WIDTH = HEADS * HEAD_DIM
DN_CONV = 4
CHUNK = 64
D_FF = 2816
FFN_CONV = 3
EPS = 1e-6

H_COLS = 3 * WIDTH + WIDTH + 3 * WIDTH + 2 * D_MODEL
COL_DN_GATE = 3 * WIDTH // 128
COL_SB_Q = COL_DN_GATE + WIDTH // 128
COL_SB_K = COL_SB_Q + HEADS
COL_SB_V = COL_SB_K + HEADS
COL_MERGE = COL_SB_V + HEADS
AB_LANES = 128
HALO = 8

VMEM_LIMIT = 56 * 1024 * 1024


def _dot(a, b):
    return jnp.dot(a, b, preferred_element_type=F32)


def _dot_nt(a, b):
    return lax.dot_general(a, b, (((1,), (1,)), ((), ())), preferred_element_type=F32)


def _dot_tn(a, b):
    return lax.dot_general(a, b, (((0,), (0,)), ((), ())), preferred_element_type=F32)


def _sigmoid(x):
    return 1.0 / (1.0 + jnp.exp(-x))


def _softplus(x):
    return jnp.maximum(x, 0.0) + jnp.log(1.0 + jnp.exp(-jnp.abs(x)))


def _in_proj_kernel(x_ref, nw_ref, w_ref, wab_ref, h_ref, ab_ref, xn_ref):
    @pl.when(pl.program_id(1) == 0)
    def _():
        x = x_ref[...]
        ms = jnp.mean(x * x, axis=-1, keepdims=True)
        xn_ref[...] = (x * lax.rsqrt(ms + EPS) * nw_ref[...]).astype(BF16)
        ab_ref[...] = _dot(xn_ref[...], wab_ref[...])

    h_ref[...] = _dot(xn_ref[...], w_ref[...]).astype(h_ref.dtype)


def _in_proj(x2, norm_w, w_main, w_ab, tm, tn):
    m = x2.shape[0]
    n = w_main.shape[1]
    return pl.pallas_call(
        _in_proj_kernel,
        name="in_proj",
        grid=(m // tm, n // tn),
        in_specs=[
            pl.BlockSpec((tm, D_MODEL), lambda i, j: (i, 0)),
            pl.BlockSpec((1, D_MODEL), lambda i, j: (0, 0)),
            pl.BlockSpec((D_MODEL, tn), lambda i, j: (0, j)),
            pl.BlockSpec((D_MODEL, AB_LANES), lambda i, j: (0, 0)),
        ],
        out_specs=[
            pl.BlockSpec((tm, tn), lambda i, j: (i, j)),
            pl.BlockSpec((tm, AB_LANES), lambda i, j: (i, 0)),
        ],
        out_shape=[
            jax.ShapeDtypeStruct((m, n), BF16),
            jax.ShapeDtypeStruct((m, AB_LANES), F32),
        ],
        scratch_shapes=[pltpu.VMEM((tm, D_MODEL), BF16)],
        compiler_params=pltpu.CompilerParams(
            dimension_semantics=("parallel", "arbitrary"),
            vmem_limit_bytes=VMEM_LIMIT),
    )(x2, norm_w, w_main, w_ab)


def _split(a):
    hi = a.astype(BF16)
    return hi, (a - hi.astype(F32)).astype(BF16)


def _mm3(a, b):
    return _dot(a[0], b[0]) + (_dot(a[0], b[1]) + _dot(a[1], b[0]))


def _gdn_kernel(qkv_ref, cw_ref, ab_ref, abt_ref, alog_row_ref, dtb_row_ref,
                alog_col_ref, dtb_col_ref, gate_ref, nw_ref, o_ref,
                s_ref, halo_ref, ext_ref, act_ref, gcol_ref, bcol_ref, grow_ref,
                u_ref, w_ref, qg_ref, kd_ref, aqk_ref, dec_ref, *, tc):
    @pl.when(pl.program_id(1) == 0)
    def _():
        s_ref[...] = jnp.zeros_like(s_ref)
        halo_ref[...] = jnp.zeros_like(halo_ref)

    x = qkv_ref[...].astype(F32)
    ext_ref[0:HALO, :] = halo_ref[...]
    ext_ref[HALO:HALO + tc, :] = x
    halo_ref[...] = x[tc - HALO:, :]
    cw = cw_ref[...]
    y = cw[3:4, :] * x
    for tap in range(DN_CONV - 1):
        shift = DN_CONV - 1 - tap
        y = y + cw[tap:tap + 1, :] * ext_ref[HALO - shift:HALO - shift + tc, :]
    act_ref[...] = y * _sigmoid(y)

    ab = ab_ref[...]
    gcol_ref[...] = -jnp.exp(alog_row_ref[...]) * _softplus(ab + dtb_row_ref[...])
    bcol_ref[...] = _sigmoid(ab)
    abt = abt_ref[...]
    grow_ref[...] = -jnp.exp(alog_col_ref[...]) * _softplus(abt[:, :HEADS, :] + dtb_col_ref[...])

    row = lax.broadcasted_iota(jnp.int32, (CHUNK, CHUNK), 0)
    col = lax.broadcasted_iota(jnp.int32, (CHUNK, CHUNK), 1)
    lower = col <= row
    strict = col < row
    eye = jnp.where(row == col, 1.0, 0.0).astype(F32)

    def same_block(log2_size):
        return (row >> log2_size) == (col >> log2_size)

    def prepare(chunks):
        probs = []
        for c in chunks:
            rows = pl.ds(pl.multiple_of(c * CHUNK, CHUNK), CHUNK)
            g_cols = gcol_ref[rows, :]
            b_cols = bcol_ref[rows, :]
            g_rows = grow_ref[c]
            for h in range(HEADS):
                probs.append(dict(c=c, rows=rows, h=h, lanes=slice(h * HEAD_DIM, (h + 1) * HEAD_DIM),
                                  g_c=g_cols[:, h:h + 1], g_r=g_rows[h:h + 1, :],
                                  beta=b_cols[:, HEADS + h:HEADS + h + 1]))
        for p in probs:
            rows, h = p["rows"], p["h"]
            q = act_ref[rows, h * HEAD_DIM:(h + 1) * HEAD_DIM]
            k = act_ref[rows, WIDTH + h * HEAD_DIM:WIDTH + (h + 1) * HEAD_DIM]
            v = act_ref[rows, 2 * WIDTH + h * HEAD_DIM:2 * WIDTH + (h + 1) * HEAD_DIM]
            q = q * (lax.rsqrt(jnp.sum(q * q, axis=-1, keepdims=True) + EPS) * (HEAD_DIM ** -0.5))
            k = k * lax.rsqrt(jnp.sum(k * k, axis=-1, keepdims=True) + EPS)
            gc_c = jnp.sum(jnp.where(lower, p["g_r"], 0.0), axis=1, keepdims=True)
            gc_r = jnp.sum(jnp.where(row <= col, p["g_c"], 0.0), axis=0, keepdims=True)
            decay = jnp.where(lower, jnp.exp(jnp.where(lower, gc_c - gc_r, 0.0)), 0.0)
            eg = jnp.exp(gc_c)
            g_last = gc_c[CHUNK - 1:CHUNK, :]
            kb = k * p["beta"]
            kk_qk = _dot_nt(jnp.concatenate([kb, q], axis=0).astype(BF16), k.astype(BF16))
            p["l"] = jnp.where(strict, kk_qk[:CHUNK] * decay, 0.0)
            p["rhs"] = _split(jnp.concatenate([v * p["beta"], kb * eg], axis=1))
            slot = p["c"] * HEADS + h
            aqk_ref[slot] = (kk_qk[CHUNK:] * decay).astype(BF16)
            dec_ref[slot] = jnp.broadcast_to(jnp.exp(g_last), (HALO, HEAD_DIM))
            qg_ref[rows, p["lanes"]] = (q * eg).astype(BF16)
            kd_ref[rows, p["lanes"]] = (k * jnp.exp(g_last - gc_c)).astype(BF16)
        invs = [eye - jnp.where(same_block(1), p["l"], 0.0) for p in probs]
        for lvl in range(1, 6):
            off_diag = same_block(lvl + 1) & jnp.logical_not(same_block(lvl))
            cs = [_split(jnp.where(off_diag, p["l"], 0.0)) for p in probs]
            ds = [_split(d) for d in invs]
            xs = [_split(_mm3(d, c)) for d, c in zip(ds, cs)]
            invs = [inv - _mm3(x, d) for inv, x, d in zip(invs, xs, ds)]
        for p, inv in zip(probs, invs):
            uw = _mm3(_split(inv), p["rhs"])
            u_ref[p["rows"], p["lanes"]] = uw[:, :HEAD_DIM]
            w_ref[p["rows"], p["lanes"]] = uw[:, HEAD_DIM:].astype(BF16)

    def prepare_body(i, carry):
        prepare([2 * i, 2 * i + 1])
        return carry

    lax.fori_loop(0, tc // (2 * CHUNK), prepare_body, 0)

    nw = nw_ref[...]

    def recur_body(c, carry):
        rows = pl.ds(pl.multiple_of(c * CHUNK, CHUNK), CHUNK)
        hs = range(HEADS)
        lanes = [slice(h * HEAD_DIM, (h + 1) * HEAD_DIM) for h in hs]
        s_old = [s_ref[h] for h in hs]
        ws_qs = [_dot(jnp.concatenate([w_ref[rows, lanes[h]], qg_ref[rows, lanes[h]]], axis=0),
                      s_old[h].astype(BF16)) for h in hs]
        v_new = [(u_ref[rows, lanes[h]] - ws_qs[h][:CHUNK]).astype(BF16) for h in hs]
        for h in hs:
            s_ref[h] = (s_old[h] * dec_ref[c * HEADS + h][0:1, :]
                        + _dot_tn(kd_ref[rows, lanes[h]], v_new[h]))
        outs = [ws_qs[h][CHUNK:] + _dot(aqk_ref[c * HEADS + h], v_new[h]) for h in hs]
        for h in hs:
            o = outs[h]
            o = o * lax.rsqrt(jnp.mean(o * o, axis=-1, keepdims=True) + EPS) * nw
            gate = gate_ref[rows, lanes[h]].astype(F32)
            o_ref[rows, lanes[h]] = (o * (gate * _sigmoid(gate))).astype(o_ref.dtype)
        return carry

    lax.fori_loop(0, tc // CHUNK, recur_body, 0)


def _gdn(h3, conv_w, ab3, abt, alog, dtb, norm_w, tc):
    b, t, _ = h3.shape
    nc = tc // CHUNK
    assert nc % 2 == 0
    alog_row = jnp.zeros((1, AB_LANES), F32).at[0, :HEADS].set(alog)
    dtb_row = jnp.zeros((1, AB_LANES), F32).at[0, :HEADS].set(dtb)
    kern = functools.partial(_gdn_kernel, tc=tc)
    return pl.pallas_call(
        kern,
        name="gdn",
        grid=(b, t // tc),
        in_specs=[
            pl.BlockSpec((None, tc, 3 * WIDTH), lambda bi, ti: (bi, ti, 0)),
            pl.BlockSpec((DN_CONV, 3 * WIDTH), lambda bi, ti: (0, 0)),
            pl.BlockSpec((None, tc, AB_LANES), lambda bi, ti: (bi, ti, 0)),
            pl.BlockSpec((None, nc, 2 * HEADS, CHUNK), lambda bi, ti: (bi, ti, 0, 0)),
            pl.BlockSpec((1, AB_LANES), lambda bi, ti: (0, 0)),
            pl.BlockSpec((1, AB_LANES), lambda bi, ti: (0, 0)),
            pl.BlockSpec((HEADS, 1), lambda bi, ti: (0, 0)),
            pl.BlockSpec((HEADS, 1), lambda bi, ti: (0, 0)),
            pl.BlockSpec((None, tc, WIDTH), lambda bi, ti: (bi, ti, COL_DN_GATE * 128 // WIDTH)),
            pl.BlockSpec((1, HEAD_DIM), lambda bi, ti: (0, 0)),
        ],
        out_specs=pl.BlockSpec((None, tc, WIDTH), lambda bi, ti: (bi, ti, 0)),
        out_shape=jax.ShapeDtypeStruct((b, t, WIDTH), BF16),
        scratch_shapes=[
            pltpu.VMEM((HEADS, HEAD_DIM, HEAD_DIM), F32),
            pltpu.VMEM((HALO, 3 * WIDTH), F32),
            pltpu.VMEM((HALO + tc, 3 * WIDTH), F32),
            pltpu.VMEM((tc, 3 * WIDTH), F32),
            pltpu.VMEM((tc, AB_LANES), F32),
            pltpu.VMEM((tc, AB_LANES), F32),
            pltpu.VMEM((nc, HEADS, CHUNK), F32),
            pltpu.VMEM((tc, WIDTH), F32),
            pltpu.VMEM((tc, WIDTH), BF16),
            pltpu.VMEM((tc, WIDTH), BF16),
            pltpu.VMEM((tc, WIDTH), BF16),
            pltpu.VMEM((nc * HEADS, CHUNK, CHUNK), BF16),
            pltpu.VMEM((nc * HEADS, HALO, HEAD_DIM), F32),
        ],
        compiler_params=pltpu.CompilerParams(
            dimension_semantics=("parallel", "arbitrary"),
            vmem_limit_bytes=VMEM_LIMIT),
    )(h3, conv_w, ab3, abt, alog_row, dtb_row, alog.reshape(HEADS, 1), dtb.reshape(HEADS, 1),
      h3, norm_w.reshape(1, HEAD_DIM))


SB_DEAD_LOG2 = -150.0
SB_Q_SCALE = HEAD_DIM ** -0.5 * math.log2(math.e)


def _sb_kernel(q_ref, k_ref, v_ref, o_ref, *, tq, nh):
    i = pl.program_id(2)
    row = lax.broadcasted_iota(jnp.int32, (tq, tq), 0)
    col = lax.broadcasted_iota(jnp.int32, (tq, tq), 1)
    later = jnp.where(row > col, 1.0, 0.0).astype(BF16)
    hs = range(nh)
    lanes = [slice(h * HEAD_DIM, (h + 1) * HEAD_DIM) for h in hs]
    qs = [q_ref[:, lanes[h]] for h in hs]

    def block(j, accs, runs, mask):
        k0 = pl.multiple_of(j * tq, tq)
        zs = [_dot_nt(qs[h], k_ref[pl.ds(k0, tq), lanes[h]]) for h in hs]
        keeps = []
        for z in zs:
            nz = -z
            keep = jnp.minimum(nz, 0.0) - jnp.log2(1.0 + jnp.exp2(jnp.minimum(z, nz)))
            keeps.append(keep if mask is None else jnp.where(mask, keep, 0.0))
        inner = [_dot(keep.astype(BF16), later) for keep in keeps]
        new_accs, new_runs = [], []
        for h in hs:
            a = jnp.exp2(zs[h] + keeps[h] + inner[h] + runs[h])
            if mask is not None:
                a = jnp.where(mask, a, 0.0)
            new_accs.append(accs[h] + _dot(a.astype(BF16), v_ref[pl.ds(k0, tq), lanes[h]]))
            new_runs.append(runs[h] + jnp.sum(keeps[h], axis=1, keepdims=True))
        return new_accs, new_runs

    accs = [jnp.zeros((tq, HEAD_DIM), F32) for _ in hs]
    runs = [jnp.zeros((tq, 1), F32) for _ in hs]
    accs, runs = block(i, accs, runs, col < row)

    def alive(runs):
        m = runs[0]
        for r in runs[1:]:
            m = jnp.maximum(m, r)
        return jnp.max(m) > SB_DEAD_LOG2

    def cond(carry):
        j, _, _, live = carry
        return jnp.logical_and(j >= 0, live)

    def body(carry):
        j, accs, runs, _ = carry
        accs, runs = block(j, list(accs), list(runs), None)
        return j - 1, tuple(accs), tuple(runs), alive(runs)

    _, accs, _, _ = lax.while_loop(cond, body, (i - 1, tuple(accs), tuple(runs), alive(runs)))
    for h in hs:
        o_ref[:, lanes[h]] = accs[h].astype(o_ref.dtype)


def _sb_attention(h3, tq, nh):
    b, t, _ = h3.shape
    kern = functools.partial(_sb_kernel, tq=tq, nh=nh)
    w = nh * HEAD_DIM
    return pl.pallas_call(
        kern,
        name="sb_attn",
        grid=(b, HEADS // nh, t // tq),
        in_specs=[
            pl.BlockSpec((None, tq, w), lambda bi, hi, qi: (bi, qi, COL_SB_Q // nh + hi)),
            pl.BlockSpec((None, t, w), lambda bi, hi, qi: (bi, 0, COL_SB_K // nh + hi)),
            pl.BlockSpec((None, t, w), lambda bi, hi, qi: (bi, 0, COL_SB_V // nh + hi)),
        ],
        out_specs=pl.BlockSpec((None, tq, w), lambda bi, hi, qi: (bi, qi, hi)),
        out_shape=jax.ShapeDtypeStruct((b, t, WIDTH), BF16),
        compiler_params=pltpu.CompilerParams(
            dimension_semantics=("parallel", "parallel", "arbitrary"),
            vmem_limit_bytes=VMEM_LIMIT),
    )(h3, h3, h3)


def _merge_kernel(x_ref, odn_ref, osb_ref, gdn_ref, gsb_ref, pa_ref, pb_ref, wo_ref, nw_ref,
                  x1_ref, n2_ref):
    pa = _dot(odn_ref[...], pa_ref[...])
    pb = _dot(osb_ref[...], pb_ref[...])
    mixed = (_sigmoid(gdn_ref[...].astype(F32)) * pa + _sigmoid(gsb_ref[...].astype(F32)) * pb)
    x1 = x_ref[...] + _dot(mixed.astype(BF16), wo_ref[...])
    x1_ref[...] = x1
    ms = jnp.mean(x1 * x1, axis=-1, keepdims=True)
    n2_ref[...] = (x1 * lax.rsqrt(ms + EPS) * nw_ref[...]).astype(n2_ref.dtype)


def _merge(x2, odn2, osb2, h2, p_dn, p_sb, w_out, norm_w, tm):
    m = x2.shape[0]
    gcol = COL_MERGE * 128 // D_MODEL
    row_spec = lambda c: pl.BlockSpec((tm, D_MODEL), lambda i: (i, c))
    w_spec = pl.BlockSpec((D_MODEL, D_MODEL), lambda i: (0, 0))
    return pl.pallas_call(
        _merge_kernel,
        name="merge",
        grid=(m // tm,),
        in_specs=[row_spec(0), row_spec(0), row_spec(0), row_spec(gcol), row_spec(gcol + 1),
                  w_spec, w_spec, w_spec, pl.BlockSpec((1, D_MODEL), lambda i: (0, 0))],
        out_specs=[row_spec(0), row_spec(0)],
        out_shape=[jax.ShapeDtypeStruct((m, D_MODEL), F32),
                   jax.ShapeDtypeStruct((m, D_MODEL), BF16)],
        compiler_params=pltpu.CompilerParams(
            dimension_semantics=("parallel",),
            vmem_limit_bytes=VMEM_LIMIT),
    )(x2, odn2, osb2, h2, h2, p_dn, p_sb, w_out, norm_w)


def _ffn_kernel(n2_ref, x1_ref, wg_ref, wu_ref, cg_ref, cu_ref, wd_ref, nw_ref, o_ref,
                acc_ref, ext_ref, carry_ref, *, tm):
    ti = pl.program_id(1)
    ki = pl.program_id(2)
    n2 = n2_ref[...]

    def conv(u, cw, slot):
        prev = carry_ref[ki, slot]
        ext_ref[0:HALO, :] = jnp.where(ti == 0, 0.0, prev)
        ext_ref[HALO:HALO + tm, :] = u
        carry_ref[ki, slot] = u[tm - HALO:, :]
        y = cw[2:3, :] * u
        y = y + cw[1:2, :] * ext_ref[HALO - 1:HALO - 1 + tm, :]
        y = y + cw[0:1, :] * ext_ref[HALO - 2:HALO - 2 + tm, :]
        return y

    gate = conv(_dot(n2, wg_ref[...]), cg_ref[...], 0)
    up = conv(_dot(n2, wu_ref[...]), cu_ref[...], 1)
    act = (gate * _sigmoid(gate) * up).astype(BF16)
    part = _dot(act, wd_ref[...])

    @pl.when(ki == 0)
    def _():
        acc_ref[...] = x1_ref[...] + part

    @pl.when(ki == pl.num_programs(2) - 1)
    def _():
        x2 = acc_ref[...] + part
        ms = jnp.mean(x2 * x2, axis=-1, keepdims=True)
        o_ref[...] = x2 * lax.rsqrt(ms + EPS) * nw_ref[...]


def _ffn(n2, x1, w_up, conv_w, w_down, norm_w, tm, tf):
    b, t, _ = n2.shape
    nk = D_FF // tf
    assert nk == 2
    kern = functools.partial(_ffn_kernel, tm=tm)
    return pl.pallas_call(
        kern,
        name="ffn",
        grid=(b, t // tm, nk),
        in_specs=[
            pl.BlockSpec((None, tm, D_MODEL), lambda bi, ti, ki: (bi, ti, 0)),
            pl.BlockSpec((None, tm, D_MODEL), lambda bi, ti, ki: (bi, ti, 0)),
            pl.BlockSpec((D_MODEL, tf), lambda bi, ti, ki: (0, ki)),
            pl.BlockSpec((D_MODEL, tf), lambda bi, ti, ki: (0, nk + ki)),
            pl.BlockSpec((FFN_CONV, tf), lambda bi, ti, ki: (0, ki)),
            pl.BlockSpec((FFN_CONV, tf), lambda bi, ti, ki: (0, nk + ki)),
            pl.BlockSpec((tf, D_MODEL), lambda bi, ti, ki: (ki, 0)),
            pl.BlockSpec((1, D_MODEL), lambda bi, ti, ki: (0, 0)),
        ],
        out_specs=pl.BlockSpec((None, tm, D_MODEL), lambda bi, ti, ki: (bi, ti, 0)),
        out_shape=jax.ShapeDtypeStruct((b, t, D_MODEL), F32),
        scratch_shapes=[
            pltpu.VMEM((tm, D_MODEL), F32),
            pltpu.VMEM((HALO + tm, tf), F32),
            pltpu.VMEM((nk, 2, HALO, tf), F32),
        ],
        compiler_params=pltpu.CompilerParams(
            dimension_semantics=("parallel", "arbitrary", "arbitrary"),
            vmem_limit_bytes=VMEM_LIMIT),
    )(n2, x1, w_up, w_up, conv_w, conv_w, w_down, norm_w)


def kernel(x, norm1_w, w_in, dn_conv_w, dn_A_log, dn_dt_bias, dn_norm_w, w_proj_dn, w_proj_sb,
           w_out, norm2_w, ffn_w_up, ffn_conv_w, ffn_w_down, norm_f_w):
    b, t, d = x.shape
    assert w_in.shape[0] == 1, "single-layer block: the final RMSNorm is fused into the FFN call"
    m = b * t
    tm = min(1024, t)
    tc = min(256, t)
    tq = min(256, t)
    qkv_end = 3 * WIDTH
    b_end = qkv_end + 2 * HEADS
    g_end = b_end + WIDTH
    wl = w_in[0]
    w_main = jnp.concatenate(
        [wl[:, :qkv_end], wl[:, b_end:g_end], wl[:, g_end:g_end + WIDTH] * SB_Q_SCALE,
         wl[:, g_end + WIDTH:]], axis=1).astype(BF16)
    w_ab = jnp.pad(wl[:, qkv_end:b_end], ((0, 0), (0, AB_LANES - 2 * HEADS))).astype(BF16)

    h2, ab2 = _in_proj(x.reshape(m, d), norm1_w[0].reshape(1, d), w_main, w_ab, tm, 1024)
    h3 = h2.reshape(b, t, H_COLS)
    ab3 = ab2.reshape(b, t, AB_LANES)
    abt = ab3[:, :, :2 * HEADS].reshape(b, t // CHUNK, CHUNK, 2 * HEADS).transpose(0, 1, 3, 2)

    o_dn = _gdn(h3, dn_conv_w[0], ab3, abt, dn_A_log[0], dn_dt_bias[0], dn_norm_w[0], tc)
    o_sb = _sb_attention(h3, tq, 2)

    x1, n2 = _merge(x.reshape(m, d), o_dn.reshape(m, WIDTH), o_sb.reshape(m, WIDTH), h2,
                    w_proj_dn[0].astype(BF16), w_proj_sb[0].astype(BF16),
                    w_out[0].astype(BF16), norm2_w[0].reshape(1, d), min(512, t))

    return _ffn(n2.reshape(b, t, d), x1.reshape(b, t, d), ffn_w_up[0].astype(BF16),
                ffn_conv_w[0], ffn_w_down[0].astype(BF16), norm_f_w.reshape(1, d),
                min(512, t), D_FF // 2)
```

```python
import functools
import math

import jax
import jax.numpy as jnp
from jax import lax
from jax.experimental import pallas as pl
from jax.experimental.pallas import tpu as pltpu

F32 = jnp.float32
BF16 = jnp.bfloat16

D_MODEL = 1024
HEADS = 8
HEAD_DIM = 128
WIDTH = HEADS * HEAD_DIM
DN_CONV = 4
CHUNK = 64
D_FF = 2816
FFN_CONV = 3
EPS = 1e-6

H_COLS = 3 * WIDTH + WIDTH + 3 * WIDTH + 2 * D_MODEL
COL_DN_GATE = 3 * WIDTH // 128
COL_SB_Q = COL_DN_GATE + WIDTH // 128
COL_SB_K = COL_SB_Q + HEADS
COL_SB_V = COL_SB_K + HEADS
COL_MERGE = COL_SB_V + HEADS
AB_LANES = 128
HALO = 8

VMEM_LIMIT = 56 * 1024 * 1024


def _dot(a, b):
    return jnp.dot(a, b, preferred_element_type=F32)


def _dot_nt(a, b):
    return lax.dot_general(a, b, (((1,), (1,)), ((), ())), preferred_element_type=F32)


def _dot_tn(a, b):
    return lax.dot_general(a, b, (((0,), (0,)), ((), ())), preferred_element_type=F32)


def _sigmoid(x):
    return 1.0 / (1.0 + jnp.exp(-x))


def _softplus(x):
    return jnp.maximum(x, 0.0) + jnp.log(1.0 + jnp.exp(-jnp.abs(x)))


def _in_proj_kernel(x_ref, nw_ref, w_ref, wab_ref, h_ref, ab_ref, xn_ref):
    @pl.when(pl.program_id(1) == 0)
    def _():
        x = x_ref[...]
        ms = jnp.mean(x * x, axis=-1, keepdims=True)
        xn_ref[...] = (x * lax.rsqrt(ms + EPS) * nw_ref[...]).astype(BF16)
        ab_ref[...] = _dot(xn_ref[...], wab_ref[...])

    h_ref[...] = _dot(xn_ref[...], w_ref[...]).astype(h_ref.dtype)


def _in_proj(x2, norm_w, w_main, w_ab, tm, tn):
    m = x2.shape[0]
    n = w_main.shape[1]
    return pl.pallas_call(
        _in_proj_kernel,
        name="in_proj",
        grid=(m // tm, n // tn),
        in_specs=[
            pl.BlockSpec((tm, D_MODEL), lambda i, j: (i, 0)),
            pl.BlockSpec((1, D_MODEL), lambda i, j: (0, 0)),
            pl.BlockSpec((D_MODEL, tn), lambda i, j: (0, j)),
            pl.BlockSpec((D_MODEL, AB_LANES), lambda i, j: (0, 0)),
        ],
        out_specs=[
            pl.BlockSpec((tm, tn), lambda i, j: (i, j)),
            pl.BlockSpec((tm, AB_LANES), lambda i, j: (i, 0)),
        ],
        out_shape=[
            jax.ShapeDtypeStruct((m, n), BF16),
            jax.ShapeDtypeStruct((m, AB_LANES), F32),
        ],
        scratch_shapes=[pltpu.VMEM((tm, D_MODEL), BF16)],
        compiler_params=pltpu.CompilerParams(
            dimension_semantics=("parallel", "arbitrary"),
            vmem_limit_bytes=VMEM_LIMIT),
    )(x2, norm_w, w_main, w_ab)


PAIRS = HEADS // 2
PREP_CHUNKS = 4


def _hi_lo(a):
    hi = a.astype(BF16)
    return hi, (a - hi.astype(F32)).astype(BF16)


def _lhs3(hi, lo):
    return jnp.concatenate([hi, hi, lo], axis=1)


def _rhs3(hi, lo):
    return jnp.concatenate([hi, lo, hi], axis=0)


def _gdn_kernel(qkv_ref, cw_ref, ab_ref, abt_ref, alog_row_ref, dtb_row_ref,
                alog_pk_ref, dtb_pk_ref, gate_ref, nw_ref, o_ref,
                s_ref, halo_ref, ext_ref, act_ref, gcol_ref, bcol_ref, grow_ref,
                u_ref, w_ref, qg_ref, kd_ref, aqk_ref, dec_ref, *, tc):
    @pl.when(pl.program_id(1) == 0)
    def _():
        s_ref[...] = jnp.zeros_like(s_ref)
        halo_ref[...] = jnp.zeros_like(halo_ref)

    for j in range(3 * WIDTH // HEAD_DIM):
        ls = slice(j * HEAD_DIM, (j + 1) * HEAD_DIM)
        x = qkv_ref[:, ls].astype(F32)
        ext_ref[0:HALO, ls] = halo_ref[:, ls]
        ext_ref[HALO:HALO + tc, ls] = x
        halo_ref[:, ls] = x[tc - HALO:, :]
        y = cw_ref[DN_CONV - 1:DN_CONV, ls] * x
        for tap in range(DN_CONV - 1):
            shift = DN_CONV - 1 - tap
            y = y + cw_ref[tap:tap + 1, ls] * ext_ref[HALO - shift:HALO - shift + tc, ls]
        act_ref[:, ls] = y * _sigmoid(y)

    ab = ab_ref[...]
    gcol_ref[...] = -jnp.exp(alog_row_ref[...]) * _softplus(ab + dtb_row_ref[...])
    bcol_ref[...] = _sigmoid(ab)
    grow_ref[...] = -jnp.exp(alog_pk_ref[...]) * _softplus(abt_ref[:, :PAIRS, :] + dtb_pk_ref[...])

    row = lax.broadcasted_iota(jnp.int32, (CHUNK, 2 * CHUNK), 0)
    lane = lax.broadcasted_iota(jnp.int32, (CHUNK, 2 * CHUNK), 1)
    col = lane & (CHUNK - 1)
    left = lane < CHUNK
    lower = col <= row
    strict = col < row
    eye = jnp.where(row == col, 1.0, 0.0).astype(F32)
    zero_bf = jnp.zeros((CHUNK, 2 * CHUNK), BF16)
    zero_tile = jnp.zeros((CHUNK, HEAD_DIM), BF16)

    def same_block(log2_size):
        return (row >> log2_size) == (col >> log2_size)

    def block_diag(x):
        return jnp.concatenate([jnp.where(left, x, zero_bf), jnp.where(left, zero_bf, x)], axis=0)

    def prepare(chunks):
        probs = []
        for c in chunks:
            rows = pl.ds(pl.multiple_of(c * CHUNK, CHUNK), CHUNK)
            g_cols = gcol_ref[rows, :]
            b_cols = bcol_ref[rows, :]
            g_rows = grow_ref[c]
            for pi in range(PAIRS):
                probs.append(dict(c=c, rows=rows, pi=pi, g_cols=g_cols, b_cols=b_cols,
                                  g_r=g_rows[pi:pi + 1, :]))
        for p in probs:
            rows, pi = p["rows"], p["pi"]
            hs = (2 * pi, 2 * pi + 1)
            masked = jnp.where(lower, p["g_r"], 0.0)
            gsum = [jnp.sum(masked[:, :CHUNK], axis=1, keepdims=True),
                    jnp.sum(masked[:, CHUNK:], axis=1, keepdims=True)]
            gc_c = jnp.where(left, gsum[0], gsum[1])
            g_c = jnp.where(left, p["g_cols"][:, hs[0]:hs[0] + 1], p["g_cols"][:, hs[1]:hs[1] + 1])
            gc_r = jnp.sum(jnp.where(row <= col, g_c, 0.0), axis=0, keepdims=True)
            decay = jnp.where(lower, jnp.exp(jnp.where(lower, gc_c - gc_r, 0.0)), 0.0)
            qs, ks, kbs, rhs_hi, rhs_lo = [], [], [], [], []
            for side, h in enumerate(hs):
                lanes = slice(h * HEAD_DIM, (h + 1) * HEAD_DIM)
                q = act_ref[rows, h * HEAD_DIM:(h + 1) * HEAD_DIM]
                k = act_ref[rows, WIDTH + h * HEAD_DIM:WIDTH + (h + 1) * HEAD_DIM]
                v = act_ref[rows, 2 * WIDTH + h * HEAD_DIM:2 * WIDTH + (h + 1) * HEAD_DIM]
                q = q * (lax.rsqrt(jnp.sum(q * q, axis=-1, keepdims=True) + EPS) * (HEAD_DIM ** -0.5))
                k = k * lax.rsqrt(jnp.sum(k * k, axis=-1, keepdims=True) + EPS)
                beta = p["b_cols"][:, HEADS + h:HEADS + h + 1]
                eg = jnp.exp(gsum[side])
                g_last = gsum[side][CHUNK - 1:CHUNK, :]
                kb = k * beta
                hi, lo = _hi_lo(jnp.concatenate([v * beta, kb * eg], axis=1))
                rhs_hi.append(hi)
                rhs_lo.append(lo)
                qs.append(q.astype(BF16))
                ks.append(k.astype(BF16))
                kbs.append(kb.astype(BF16))
                dec_ref[p["c"] * HEADS + h] = jnp.broadcast_to(jnp.exp(g_last), (HALO, HEAD_DIM))
                qg_ref[rows, lanes] = (q * eg).astype(BF16)
                kd_ref[rows, lanes] = (k * jnp.exp(g_last - gsum[side])).astype(BF16)
            p["rhs"] = _rhs3(jnp.concatenate(rhs_hi, axis=0), jnp.concatenate(rhs_lo, axis=0))
            kq = jnp.concatenate([jnp.concatenate(kbs, axis=1), jnp.concatenate(qs, axis=1)], axis=0)
            kt = jnp.concatenate([jnp.concatenate([ks[0], zero_tile], axis=1),
                                  jnp.concatenate([zero_tile, ks[1]], axis=1)], axis=0)
            kk_qk = _dot_nt(kq, kt)
            p["l"] = jnp.where(strict, kk_qk[:CHUNK] * decay, 0.0)
            aqk_ref[p["c"] * PAIRS + pi] = (kk_qk[CHUNK:] * decay).astype(BF16)
        invs = [eye - jnp.where(same_block(1), p["l"], 0.0) for p in probs]
        for lvl in range(1, 6):
            off_diag = same_block(lvl + 1) & jnp.logical_not(same_block(lvl))
            cs = [_hi_lo(jnp.where(off_diag, p["l"], 0.0)) for p in probs]
            ds = [_hi_lo(d) for d in invs]
            xs = [_hi_lo(_dot(_lhs3(*d), _rhs3(block_diag(c[0]), block_diag(c[1]))))
                  for d, c in zip(ds, cs)]
            invs = [inv - _dot(_lhs3(*x), _rhs3(block_diag(d[0]), block_diag(d[1])))
                    for inv, x, d in zip(invs, xs, ds)]
        for p, inv in zip(probs, invs):
            hi, lo = _hi_lo(inv)
            for side in range(2):
                h = 2 * p["pi"] + side
                lanes = slice(h * HEAD_DIM, (h + 1) * HEAD_DIM)
                keep = left if side == 0 else jnp.logical_not(left)
                uw = _dot(_lhs3(jnp.where(keep, hi, zero_bf), jnp.where(keep, lo, zero_bf)), p["rhs"])
                u_ref[p["rows"], lanes] = uw[:, :HEAD_DIM]
                w_ref[p["rows"], lanes] = uw[:, HEAD_DIM:].astype(BF16)

    def prepare_body(i, carry):
        prepare([PREP_CHUNKS * i + j for j in range(PREP_CHUNKS)])
        return carry

    lax.fori_loop(0, tc // (PREP_CHUNKS * CHUNK), prepare_body, 0)

    nw = nw_ref[...]

    def recur_body(c, carry):
        rows = pl.ds(pl.multiple_of(c * CHUNK, CHUNK), CHUNK)
        hs = range(HEADS)
        lanes = [slice(h * HEAD_DIM, (h + 1) * HEAD_DIM) for h in hs]
        s_old = [s_ref[h] for h in hs]
        ws_qs = [_dot(jnp.concatenate([w_ref[rows, lanes[h]], qg_ref[rows, lanes[h]]], axis=0),
                      s_old[h].astype(BF16)) for h in hs]
        v_new = [(u_ref[rows, lanes[h]] - ws_qs[h][:CHUNK]).astype(BF16) for h in hs]
        for h in hs:
            s_ref[h] = (s_old[h] * dec_ref[c * HEADS + h][0:1, :]
                        + _dot_tn(kd_ref[rows, lanes[h]], v_new[h]))
        intra = []
        for pi in range(PAIRS):
            v_bd = jnp.concatenate([jnp.concatenate([v_new[2 * pi], zero_tile], axis=1),
                                    jnp.concatenate([zero_tile, v_new[2 * pi + 1]], axis=1)], axis=0)
            av = _dot(aqk_ref[c * PAIRS + pi], v_bd)
            intra += [av[:, :HEAD_DIM], av[:, HEAD_DIM:]]
        for h in hs:
            o = ws_qs[h][CHUNK:] + intra[h]
            o = o * lax.rsqrt(jnp.mean(o * o, axis=-1, keepdims=True) + EPS) * nw
            gate = gate_ref[rows, lanes[h]].astype(F32)
            o_ref[rows, lanes[h]] = (o * (gate * _sigmoid(gate))).astype(o_ref.dtype)
        return carry

    lax.fori_loop(0, tc // CHUNK, recur_body, 0)


def _gdn(h3, conv_w, ab3, abt, alog, dtb, norm_w, tc):
    b, t, _ = h3.shape
    nc = tc // CHUNK
    assert nc % PREP_CHUNKS == 0
    alog_row = jnp.zeros((1, AB_LANES), F32).at[0, :HEADS].set(alog)
    dtb_row = jnp.zeros((1, AB_LANES), F32).at[0, :HEADS].set(dtb)
    alog_pk = jnp.repeat(alog, CHUNK).reshape(PAIRS, 2 * CHUNK)
    dtb_pk = jnp.repeat(dtb, CHUNK).reshape(PAIRS, 2 * CHUNK)
    kern = functools.partial(_gdn_kernel, tc=tc)
    return pl.pallas_call(
        kern,
        name="gdn",
        grid=(b, t // tc),
        in_specs=[
            pl.BlockSpec((None, tc, 3 * WIDTH), lambda bi, ti: (bi, ti, 0)),
            pl.BlockSpec((DN_CONV, 3 * WIDTH), lambda bi, ti: (0, 0)),
            pl.BlockSpec((None, tc, AB_LANES), lambda bi, ti: (bi, ti, 0)),
            pl.BlockSpec((None, nc, HEADS, 2 * CHUNK), lambda bi, ti: (bi, ti, 0, 0)),
            pl.BlockSpec((1, AB_LANES), lambda bi, ti: (0, 0)),
            pl.BlockSpec((1, AB_LANES), lambda bi, ti: (0, 0)),
            pl.BlockSpec((PAIRS, 2 * CHUNK), lambda bi, ti: (0, 0)),
            pl.BlockSpec((PAIRS, 2 * CHUNK), lambda bi, ti: (0, 0)),
            pl.BlockSpec((None, tc, WIDTH), lambda bi, ti: (bi, ti, COL_DN_GATE * 128 // WIDTH)),
            pl.BlockSpec((1, HEAD_DIM), lambda bi, ti: (0, 0)),
        ],
        out_specs=pl.BlockSpec((None, tc, WIDTH), lambda bi, ti: (bi, ti, 0)),
        out_shape=jax.ShapeDtypeStruct((b, t, WIDTH), BF16),
        scratch_shapes=[
            pltpu.VMEM((HEADS, HEAD_DIM, HEAD_DIM), F32),
            pltpu.VMEM((HALO, 3 * WIDTH), F32),
            pltpu.VMEM((HALO + tc, 3 * WIDTH), F32),
            pltpu.VMEM((tc, 3 * WIDTH), F32),
            pltpu.VMEM((tc, AB_LANES), F32),
            pltpu.VMEM((tc, AB_LANES), F32),
            pltpu.VMEM((nc, PAIRS, 2 * CHUNK), F32),
            pltpu.VMEM((tc, WIDTH), F32),
            pltpu.VMEM((tc, WIDTH), BF16),
            pltpu.VMEM((tc, WIDTH), BF16),
            pltpu.VMEM((tc, WIDTH), BF16),
            pltpu.VMEM((nc * PAIRS, CHUNK, 2 * CHUNK), BF16),
            pltpu.VMEM((nc * HEADS, HALO, HEAD_DIM), F32),
        ],
        compiler_params=pltpu.CompilerParams(
            dimension_semantics=("parallel", "arbitrary"),
            vmem_limit_bytes=VMEM_LIMIT),
    )(h3, conv_w, ab3, abt, alog_row, dtb_row, alog_pk, dtb_pk, h3, norm_w.reshape(1, HEAD_DIM))


SB_DEAD_LOG2 = -150.0
SB_Q_SCALE = HEAD_DIM ** -0.5 * math.log2(math.e)


def _sb_kernel(q_ref, k_ref, v_ref, o_ref, *, tq, nh):
    i = pl.program_id(2)
    row = lax.broadcasted_iota(jnp.int32, (tq, tq), 0)
    col = lax.broadcasted_iota(jnp.int32, (tq, tq), 1)
    later = jnp.where(row > col, 1.0, 0.0).astype(BF16)
    hs = range(nh)
    lanes = [slice(h * HEAD_DIM, (h + 1) * HEAD_DIM) for h in hs]
    qs = [q_ref[:, lanes[h]] for h in hs]

    def block(j, accs, runs, mask):
        k0 = pl.multiple_of(j * tq, tq)
        zs = [_dot_nt(qs[h], k_ref[pl.ds(k0, tq), lanes[h]]) for h in hs]
        keeps = []
        for z in zs:
            nz = -z
            keep = jnp.minimum(nz, 0.0) - jnp.log2(1.0 + jnp.exp2(jnp.minimum(z, nz)))
            keeps.append(keep if mask is None else jnp.where(mask, keep, 0.0))
        inner = [_dot(keep.astype(BF16), later) for keep in keeps]
        new_accs, new_runs = [], []
        for h in hs:
            a = jnp.exp2(zs[h] + keeps[h] + inner[h] + runs[h])
            if mask is not None:
                a = jnp.where(mask, a, 0.0)
            new_accs.append(accs[h] + _dot(a.astype(BF16), v_ref[pl.ds(k0, tq), lanes[h]]))
            new_runs.append(runs[h] + jnp.sum(keeps[h], axis=1, keepdims=True))
        return new_accs, new_runs

    accs = [jnp.zeros((tq, HEAD_DIM), F32) for _ in hs]
    runs = [jnp.zeros((tq, 1), F32) for _ in hs]
    accs, runs = block(i, accs, runs, col < row)

    def alive(runs):
        m = runs[0]
        for r in runs[1:]:
            m = jnp.maximum(m, r)
        return jnp.max(m) > SB_DEAD_LOG2

    def cond(carry):
        j, _, _, live = carry
        return jnp.logical_and(j >= 0, live)

    def body(carry):
        j, accs, runs, _ = carry
        accs, runs = block(j, list(accs), list(runs), None)
        return j - 1, tuple(accs), tuple(runs), alive(runs)

    _, accs, _, _ = lax.while_loop(cond, body, (i - 1, tuple(accs), tuple(runs), alive(runs)))
    for h in hs:
        o_ref[:, lanes[h]] = accs[h].astype(o_ref.dtype)


def _sb_attention(h3, tq, nh):
    b, t, _ = h3.shape
    kern = functools.partial(_sb_kernel, tq=tq, nh=nh)
    w = nh * HEAD_DIM
    return pl.pallas_call(
        kern,
        name="sb_attn",
        grid=(b, HEADS // nh, t // tq),
        in_specs=[
            pl.BlockSpec((None, tq, w), lambda bi, hi, qi: (bi, qi, COL_SB_Q // nh + hi)),
            pl.BlockSpec((None, t, w), lambda bi, hi, qi: (bi, 0, COL_SB_K // nh + hi)),
            pl.BlockSpec((None, t, w), lambda bi, hi, qi: (bi, 0, COL_SB_V // nh + hi)),
        ],
        out_specs=pl.BlockSpec((None, tq, w), lambda bi, hi, qi: (bi, qi, hi)),
        out_shape=jax.ShapeDtypeStruct((b, t, WIDTH), BF16),
        compiler_params=pltpu.CompilerParams(
            dimension_semantics=("parallel", "parallel", "arbitrary"),
            vmem_limit_bytes=VMEM_LIMIT),
    )(h3, h3, h3)


def _merge_kernel(x_ref, odn_ref, osb_ref, gdn_ref, gsb_ref, pa_ref, pb_ref, wo_ref, nw_ref,
                  x1_ref, n2_ref):
    pa = _dot(odn_ref[...], pa_ref[...])
    pb = _dot(osb_ref[...], pb_ref[...])
    mixed = (_sigmoid(gdn_ref[...].astype(F32)) * pa + _sigmoid(gsb_ref[...].astype(F32)) * pb)
    x1 = x_ref[...] + _dot(mixed.astype(BF16), wo_ref[...])
    x1_ref[...] = x1
    ms = jnp.mean(x1 * x1, axis=-1, keepdims=True)
    n2_ref[...] = (x1 * lax.rsqrt(ms + EPS) * nw_ref[...]).astype(n2_ref.dtype)


def _merge(x2, odn2, osb2, h2, p_dn, p_sb, w_out, norm_w, tm):
    m = x2.shape[0]
    gcol = COL_MERGE * 128 // D_MODEL
    row_spec = lambda c: pl.BlockSpec((tm, D_MODEL), lambda i: (i, c))
    w_spec = pl.BlockSpec((D_MODEL, D_MODEL), lambda i: (0, 0))
    return pl.pallas_call(
        _merge_kernel,
        name="merge",
        grid=(m // tm,),
        in_specs=[row_spec(0), row_spec(0), row_spec(0), row_spec(gcol), row_spec(gcol + 1),
                  w_spec, w_spec, w_spec, pl.BlockSpec((1, D_MODEL), lambda i: (0, 0))],
        out_specs=[row_spec(0), row_spec(0)],
        out_shape=[jax.ShapeDtypeStruct((m, D_MODEL), F32),
                   jax.ShapeDtypeStruct((m, D_MODEL), BF16)],
        compiler_params=pltpu.CompilerParams(
            dimension_semantics=("parallel",),
            vmem_limit_bytes=VMEM_LIMIT),
    )(x2, odn2, osb2, h2, h2, p_dn, p_sb, w_out, norm_w)


def _ffn_kernel(n2_ref, x1_ref, wg_ref, wu_ref, cg_ref, cu_ref, wd_ref, nw_ref, o_ref,
                acc_ref, ext_ref, carry_ref, *, tm):
    ti = pl.program_id(1)
    ki = pl.program_id(2)
    n2 = n2_ref[...]

    def conv(u, cw, slot):
        prev = carry_ref[ki, slot]
        ext_ref[0:HALO, :] = jnp.where(ti == 0, 0.0, prev)
        ext_ref[HALO:HALO + tm, :] = u
        carry_ref[ki, slot] = u[tm - HALO:, :]
        y = cw[2:3, :] * u
        y = y + cw[1:2, :] * ext_ref[HALO - 1:HALO - 1 + tm, :]
        y = y + cw[0:1, :] * ext_ref[HALO - 2:HALO - 2 + tm, :]
        return y

    gate = conv(_dot(n2, wg_ref[...]), cg_ref[...], 0)
    up = conv(_dot(n2, wu_ref[...]), cu_ref[...], 1)
    act = (gate * _sigmoid(gate) * up).astype(BF16)
    part = _dot(act, wd_ref[...])

    @pl.when(ki == 0)
    def _():
        acc_ref[...] = x1_ref[...] + part

    @pl.when(ki == pl.num_programs(2) - 1)
    def _():
        x2 = acc_ref[...] + part
        ms = jnp.mean(x2 * x2, axis=-1, keepdims=True)
        o_ref[...] = x2 * lax.rsqrt(ms + EPS) * nw_ref[...]


def _ffn(n2, x1, w_up, conv_w, w_down, norm_w, tm, tf):
    b, t, _ = n2.shape
    nk = D_FF // tf
    assert nk == 2
    kern = functools.partial(_ffn_kernel, tm=tm)
    return pl.pallas_call(
        kern,
        name="ffn",
        grid=(b, t // tm, nk),
        in_specs=[
            pl.BlockSpec((None, tm, D_MODEL), lambda bi, ti, ki: (bi, ti, 0)),
            pl.BlockSpec((None, tm, D_MODEL), lambda bi, ti, ki: (bi, ti, 0)),
            pl.BlockSpec((D_MODEL, tf), lambda bi, ti, ki: (0, ki)),
            pl.BlockSpec((D_MODEL, tf), lambda bi, ti, ki: (0, nk + ki)),
            pl.BlockSpec((FFN_CONV, tf), lambda bi, ti, ki: (0, ki)),
            pl.BlockSpec((FFN_CONV, tf), lambda bi, ti, ki: (0, nk + ki)),
            pl.BlockSpec((tf, D_MODEL), lambda bi, ti, ki: (ki, 0)),
            pl.BlockSpec((1, D_MODEL), lambda bi, ti, ki: (0, 0)),
        ],
        out_specs=pl.BlockSpec((None, tm, D_MODEL), lambda bi, ti, ki: (bi, ti, 0)),
        out_shape=jax.ShapeDtypeStruct((b, t, D_MODEL), F32),
        scratch_shapes=[
            pltpu.VMEM((tm, D_MODEL), F32),
            pltpu.VMEM((HALO + tm, tf), F32),
            pltpu.VMEM((nk, 2, HALO, tf), F32),
        ],
        compiler_params=pltpu.CompilerParams(
            dimension_semantics=("parallel", "arbitrary", "arbitrary"),
            vmem_limit_bytes=VMEM_LIMIT),
    )(n2, x1, w_up, w_up, conv_w, conv_w, w_down, norm_w)


def kernel(x, norm1_w, w_in, dn_conv_w, dn_A_log, dn_dt_bias, dn_norm_w, w_proj_dn, w_proj_sb,
           w_out, norm2_w, ffn_w_up, ffn_conv_w, ffn_w_down, norm_f_w):
    b, t, d = x.shape
    assert w_in.shape[0] == 1, "single-layer block: the final RMSNorm is fused into the FFN call"
    m = b * t
    tm = min(1024, t)
    tc = min(256, t)
    tq = min(256, t)
    qkv_end = 3 * WIDTH
    b_end = qkv_end + 2 * HEADS
    g_end = b_end + WIDTH
    wl = w_in[0]
    w_main = jnp.concatenate(
        [wl[:, :qkv_end], wl[:, b_end:g_end], wl[:, g_end:g_end + WIDTH] * SB_Q_SCALE,
         wl[:, g_end + WIDTH:]], axis=1).astype(BF16)
    w_ab = jnp.pad(wl[:, qkv_end:b_end], ((0, 0), (0, AB_LANES - 2 * HEADS))).astype(BF16)

    h2, ab2 = _in_proj(x.reshape(m, d), norm1_w[0].reshape(1, d), w_main, w_ab, tm, H_COLS // 4)
    h3 = h2.reshape(b, t, H_COLS)
    ab3 = ab2.reshape(b, t, AB_LANES)
    abt = (ab3[:, :, :2 * HEADS].reshape(b, t // CHUNK, CHUNK, 2 * HEADS).transpose(0, 1, 3, 2)
           .reshape(b, t // CHUNK, HEADS, 2 * CHUNK))

    o_dn = _gdn(h3, dn_conv_w[0], ab3, abt, dn_A_log[0], dn_dt_bias[0], dn_norm_w[0], tc)
    o_sb = _sb_attention(h3, tq, 2)

    x1, n2 = _merge(x.reshape(m, d), o_dn.reshape(m, WIDTH), o_sb.reshape(m, WIDTH), h2,
                    w_proj_dn[0].astype(BF16), w_proj_sb[0].astype(BF16),
                    w_out[0].astype(BF16), norm2_w[0].reshape(1, d), min(512, t))

    return _ffn(n2.reshape(b, t, d), x1.reshape(b, t, d), ffn_w_up[0].astype(BF16),
                ffn_conv_w[0], ffn_w_down[0].astype(BF16), norm_f_w.reshape(1, d),
                min(512, t), D_FF // 2)
```

```python
import functools
import math

import jax
import jax.numpy as jnp
from jax import lax
from jax.experimental import pallas as pl
from jax.experimental.pallas import tpu as pltpu

F32 = jnp.float32
BF16 = jnp.bfloat16

D_MODEL = 1024
HEADS = 8
HEAD_DIM = 128
WIDTH = HEADS * HEAD_DIM
DN_CONV = 4
CHUNK = 64
D_FF = 2816
FFN_CONV = 3
EPS = 1e-6

H_COLS = 3 * WIDTH + WIDTH + 3 * WIDTH + 2 * D_MODEL
COL_DN_GATE = 3 * WIDTH // 128
COL_SB_Q = COL_DN_GATE + WIDTH // 128
COL_SB_K = COL_SB_Q + HEADS
COL_SB_V = COL_SB_K + HEADS
COL_MERGE = COL_SB_V + HEADS
AB_LANES = 128
HALO = 8

VMEM_LIMIT = 56 * 1024 * 1024


def _dot(a, b):
    return jnp.dot(a, b, preferred_element_type=F32)


def _dot_nt(a, b):
    return lax.dot_general(a, b, (((1,), (1,)), ((), ())), preferred_element_type=F32)


def _dot_tn(a, b):
    return lax.dot_general(a, b, (((0,), (0,)), ((), ())), preferred_element_type=F32)


def _sigmoid(x):
    return 1.0 / (1.0 + jnp.exp(-x))


def _softplus(x):
    return jnp.maximum(x, 0.0) + jnp.log(1.0 + jnp.exp(-jnp.abs(x)))


def _in_proj_kernel(x_ref, nw_ref, w_ref, wab_ref, h_ref, ab_ref, xn_ref):
    @pl.when(pl.program_id(1) == 0)
    def _():
        x = x_ref[...]
        ms = jnp.mean(x * x, axis=-1, keepdims=True)
        xn_ref[...] = (x * lax.rsqrt(ms + EPS) * nw_ref[...]).astype(BF16)
        ab_ref[...] = _dot(xn_ref[...], wab_ref[...])

    h_ref[...] = _dot(xn_ref[...], w_ref[...]).astype(h_ref.dtype)


def _in_proj(x2, norm_w, w_main, w_ab, tm, tn):
    m = x2.shape[0]
    n = w_main.shape[1]
    return pl.pallas_call(
        _in_proj_kernel,
        name="in_proj",
        grid=(m // tm, n // tn),
        in_specs=[
            pl.BlockSpec((tm, D_MODEL), lambda i, j: (i, 0)),
            pl.BlockSpec((1, D_MODEL), lambda i, j: (0, 0)),
            pl.BlockSpec((D_MODEL, tn), lambda i, j: (0, j)),
            pl.BlockSpec((D_MODEL, AB_LANES), lambda i, j: (0, 0)),
        ],
        out_specs=[
            pl.BlockSpec((tm, tn), lambda i, j: (i, j)),
            pl.BlockSpec((tm, AB_LANES), lambda i, j: (i, 0)),
        ],
        out_shape=[
            jax.ShapeDtypeStruct((m, n), BF16),
            jax.ShapeDtypeStruct((m, AB_LANES), F32),
        ],
        scratch_shapes=[pltpu.VMEM((tm, D_MODEL), BF16)],
        compiler_params=pltpu.CompilerParams(
            dimension_semantics=("parallel", "arbitrary"),
            vmem_limit_bytes=VMEM_LIMIT),
    )(x2, norm_w, w_main, w_ab)


PAIRS = HEADS // 2
PREP_CHUNKS = 4


def _hi_lo(a):
    hi = a.astype(BF16)
    return hi, (a - hi.astype(F32)).astype(BF16)


def _lhs3(hi, lo):
    return jnp.concatenate([hi, hi, lo], axis=1)


def _rhs3(hi, lo):
    return jnp.concatenate([hi, lo, hi], axis=0)


def _gdn_kernel(qkv_ref, cw_ref, ab_ref, abt_ref, alog_row_ref, dtb_row_ref,
                alog_pk_ref, dtb_pk_ref, gate_ref, nw_ref, o_ref,
                s_ref, halo_ref, ext_ref, act_ref, gcol_ref, bcol_ref, grow_ref,
                u_ref, w_ref, qg_ref, kd_ref, aqk_ref, dec_ref, *, tc):
    @pl.when(pl.program_id(1) == 0)
    def _():
        s_ref[...] = jnp.zeros_like(s_ref)
        halo_ref[...] = jnp.zeros_like(halo_ref)

    for j in range(3 * WIDTH // HEAD_DIM):
        ls = slice(j * HEAD_DIM, (j + 1) * HEAD_DIM)
        x = qkv_ref[:, ls].astype(F32)
        ext_ref[0:HALO, ls] = halo_ref[:, ls]
        ext_ref[HALO:HALO + tc, ls] = x
        halo_ref[:, ls] = x[tc - HALO:, :]
        y = cw_ref[DN_CONV - 1:DN_CONV, ls] * x
        for tap in range(DN_CONV - 1):
            shift = DN_CONV - 1 - tap
            y = y + cw_ref[tap:tap + 1, ls] * ext_ref[HALO - shift:HALO - shift + tc, ls]
        act_ref[:, ls] = y * _sigmoid(y)

    ab = ab_ref[...]
    gcol_ref[...] = -jnp.exp(alog_row_ref[...]) * _softplus(ab + dtb_row_ref[...])
    bcol_ref[...] = _sigmoid(ab)
    grow_ref[...] = -jnp.exp(alog_pk_ref[...]) * _softplus(abt_ref[:, :PAIRS, :] + dtb_pk_ref[...])

    row = lax.broadcasted_iota(jnp.int32, (CHUNK, 2 * CHUNK), 0)
    lane = lax.broadcasted_iota(jnp.int32, (CHUNK, 2 * CHUNK), 1)
    col = lane & (CHUNK - 1)
    left = lane < CHUNK
    lower = col <= row
    strict = col < row
    eye = jnp.where(row == col, 1.0, 0.0).astype(F32)
    zero_bf = jnp.zeros((CHUNK, 2 * CHUNK), BF16)
    zero_tile = jnp.zeros((CHUNK, HEAD_DIM), BF16)

    def same_block(log2_size):
        return (row >> log2_size) == (col >> log2_size)

    def block_diag(x):
        return jnp.concatenate([jnp.where(left, x, zero_bf), jnp.where(left, zero_bf, x)], axis=0)

    def prepare(chunks):
        probs = []
        for c in chunks:
            rows = pl.ds(pl.multiple_of(c * CHUNK, CHUNK), CHUNK)
            g_cols = gcol_ref[rows, :]
            b_cols = bcol_ref[rows, :]
            g_rows = grow_ref[c]
            for pi in range(PAIRS):
                probs.append(dict(c=c, rows=rows, pi=pi, g_cols=g_cols, b_cols=b_cols,
                                  g_r=g_rows[pi:pi + 1, :]))
        for p in probs:
            rows, pi = p["rows"], p["pi"]
            hs = (2 * pi, 2 * pi + 1)
            masked = jnp.where(lower, p["g_r"], 0.0)
            gsum = [jnp.sum(masked[:, :CHUNK], axis=1, keepdims=True),
                    jnp.sum(masked[:, CHUNK:], axis=1, keepdims=True)]
            gc_c = jnp.where(left, gsum[0], gsum[1])
            g_c = jnp.where(left, p["g_cols"][:, hs[0]:hs[0] + 1], p["g_cols"][:, hs[1]:hs[1] + 1])
            gc_r = jnp.sum(jnp.where(row <= col, g_c, 0.0), axis=0, keepdims=True)
            decay = jnp.where(lower, jnp.exp(jnp.where(lower, gc_c - gc_r, 0.0)), 0.0)
            qs, ks, kbs, rhs_hi, rhs_lo = [], [], [], [], []
            for side, h in enumerate(hs):
                lanes = slice(h * HEAD_DIM, (h + 1) * HEAD_DIM)
                q = act_ref[rows, h * HEAD_DIM:(h + 1) * HEAD_DIM]
                k = act_ref[rows, WIDTH + h * HEAD_DIM:WIDTH + (h + 1) * HEAD_DIM]
                v = act_ref[rows, 2 * WIDTH + h * HEAD_DIM:2 * WIDTH + (h + 1) * HEAD_DIM]
                q = q * (lax.rsqrt(jnp.sum(q * q, axis=-1, keepdims=True) + EPS) * (HEAD_DIM ** -0.5))
                k = k * lax.rsqrt(jnp.sum(k * k, axis=-1, keepdims=True) + EPS)
                beta = p["b_cols"][:, HEADS + h:HEADS + h + 1]
                eg = jnp.exp(gsum[side])
                g_last = gsum[side][CHUNK - 1:CHUNK, :]
                kb = k * beta
                hi, lo = _hi_lo(jnp.concatenate([v * beta, kb * eg], axis=1))
                rhs_hi.append(hi)
                rhs_lo.append(lo)
                qs.append(q.astype(BF16))
                ks.append(k.astype(BF16))
                kbs.append(kb.astype(BF16))
                dec_ref[p["c"] * HEADS + h] = jnp.broadcast_to(jnp.exp(g_last), (HALO, HEAD_DIM))
                qg_ref[rows, lanes] = (q * eg).astype(BF16)
                kd_ref[rows, lanes] = (k * jnp.exp(g_last - gsum[side])).astype(BF16)
            p["rhs"] = _rhs3(jnp.concatenate(rhs_hi, axis=0), jnp.concatenate(rhs_lo, axis=0))
            kq = jnp.concatenate([jnp.concatenate(kbs, axis=1), jnp.concatenate(qs, axis=1)], axis=0)
            kt = jnp.concatenate([jnp.concatenate([ks[0], zero_tile], axis=1),
                                  jnp.concatenate([zero_tile, ks[1]], axis=1)], axis=0)
            kk_qk = _dot_nt(kq, kt)
            p["l"] = jnp.where(strict, kk_qk[:CHUNK] * decay, 0.0)
            aqk_ref[p["c"] * PAIRS + pi] = (kk_qk[CHUNK:] * decay).astype(BF16)
        invs = [eye - jnp.where(same_block(1), p["l"], 0.0) for p in probs]
        for lvl in range(1, 6):
            off_diag = same_block(lvl + 1) & jnp.logical_not(same_block(lvl))
            cs = [_hi_lo(jnp.where(off_diag, p["l"], 0.0)) for p in probs]
            ds = [_hi_lo(d) for d in invs]
            xs = [_hi_lo(_dot(_lhs3(*d), _rhs3(block_diag(c[0]), block_diag(c[1]))))
                  for d, c in zip(ds, cs)]
            invs = [inv - _dot(_lhs3(*x), _rhs3(block_diag(d[0]), block_diag(d[1])))
                    for inv, x, d in zip(invs, xs, ds)]
        for p, inv in zip(probs, invs):
            hi, lo = _hi_lo(inv)
            for side in range(2):
                h = 2 * p["pi"] + side
                lanes = slice(h * HEAD_DIM, (h + 1) * HEAD_DIM)
                keep = left if side == 0 else jnp.logical_not(left)
                uw = _dot(_lhs3(jnp.where(keep, hi, zero_bf), jnp.where(keep, lo, zero_bf)), p["rhs"])
                u_ref[p["rows"], lanes] = uw[:, :HEAD_DIM]
                w_ref[p["rows"], lanes] = uw[:, HEAD_DIM:].astype(BF16)

    def prepare_body(i, carry):
        prepare([PREP_CHUNKS * i + j for j in range(PREP_CHUNKS)])
        return carry

    lax.fori_loop(0, tc // (PREP_CHUNKS * CHUNK), prepare_body, 0)

    nw = nw_ref[...]

    def recur_body(c, carry):
        rows = pl.ds(pl.multiple_of(c * CHUNK, CHUNK), CHUNK)
        hs = range(HEADS)
        lanes = [slice(h * HEAD_DIM, (h + 1) * HEAD_DIM) for h in hs]
        s_old = [s_ref[h] for h in hs]
        ws_qs = [_dot(jnp.concatenate([w_ref[rows, lanes[h]], qg_ref[rows, lanes[h]]], axis=0),
                      s_old[h].astype(BF16)) for h in hs]
        v_new = [(u_ref[rows, lanes[h]] - ws_qs[h][:CHUNK]).astype(BF16) for h in hs]
        for h in hs:
            s_ref[h] = (s_old[h] * dec_ref[c * HEADS + h][0:1, :]
                        + _dot_tn(kd_ref[rows, lanes[h]], v_new[h]))
        intra = []
        for pi in range(PAIRS):
            v_bd = jnp.concatenate([jnp.concatenate([v_new[2 * pi], zero_tile], axis=1),
                                    jnp.concatenate([zero_tile, v_new[2 * pi + 1]], axis=1)], axis=0)
            av = _dot(aqk_ref[c * PAIRS + pi], v_bd)
            intra += [av[:, :HEAD_DIM], av[:, HEAD_DIM:]]
        for h in hs:
            o = ws_qs[h][CHUNK:] + intra[h]
            o = o * lax.rsqrt(jnp.mean(o * o, axis=-1, keepdims=True) + EPS) * nw
            gate = gate_ref[rows, lanes[h]].astype(F32)
            o_ref[rows, lanes[h]] = (o * (gate * _sigmoid(gate))).astype(o_ref.dtype)
        return carry

    lax.fori_loop(0, tc // CHUNK, recur_body, 0)


def _gdn(h3, conv_w, ab3, abt, alog, dtb, norm_w, tc):
    b, t, _ = h3.shape
    nc = tc // CHUNK
    assert nc % PREP_CHUNKS == 0
    alog_row = jnp.zeros((1, AB_LANES), F32).at[0, :HEADS].set(alog)
    dtb_row = jnp.zeros((1, AB_LANES), F32).at[0, :HEADS].set(dtb)
    alog_pk = jnp.repeat(alog, CHUNK).reshape(PAIRS, 2 * CHUNK)
    dtb_pk = jnp.repeat(dtb, CHUNK).reshape(PAIRS, 2 * CHUNK)
    kern = functools.partial(_gdn_kernel, tc=tc)
    return pl.pallas_call(
        kern,
        name="gdn",
        grid=(b, t // tc),
        in_specs=[
            pl.BlockSpec((None, tc, 3 * WIDTH), lambda bi, ti: (bi, ti, 0)),
            pl.BlockSpec((DN_CONV, 3 * WIDTH), lambda bi, ti: (0, 0)),
            pl.BlockSpec((None, tc, AB_LANES), lambda bi, ti: (bi, ti, 0)),
            pl.BlockSpec((None, nc, HEADS, 2 * CHUNK), lambda bi, ti: (bi, ti, 0, 0)),
            pl.BlockSpec((1, AB_LANES), lambda bi, ti: (0, 0)),
            pl.BlockSpec((1, AB_LANES), lambda bi, ti: (0, 0)),
            pl.BlockSpec((PAIRS, 2 * CHUNK), lambda bi, ti: (0, 0)),
            pl.BlockSpec((PAIRS, 2 * CHUNK), lambda bi, ti: (0, 0)),
            pl.BlockSpec((None, tc, WIDTH), lambda bi, ti: (bi, ti, COL_DN_GATE * 128 // WIDTH)),
            pl.BlockSpec((1, HEAD_DIM), lambda bi, ti: (0, 0)),
        ],
        out_specs=pl.BlockSpec((None, tc, WIDTH), lambda bi, ti: (bi, ti, 0)),
        out_shape=jax.ShapeDtypeStruct((b, t, WIDTH), BF16),
        scratch_shapes=[
            pltpu.VMEM((HEADS, HEAD_DIM, HEAD_DIM), F32),
            pltpu.VMEM((HALO, 3 * WIDTH), F32),
            pltpu.VMEM((HALO + tc, 3 * WIDTH), F32),
            pltpu.VMEM((tc, 3 * WIDTH), F32),
            pltpu.VMEM((tc, AB_LANES), F32),
            pltpu.VMEM((tc, AB_LANES), F32),
            pltpu.VMEM((nc, PAIRS, 2 * CHUNK), F32),
            pltpu.VMEM((tc, WIDTH), F32),
            pltpu.VMEM((tc, WIDTH), BF16),
            pltpu.VMEM((tc, WIDTH), BF16),
            pltpu.VMEM((tc, WIDTH), BF16),
            pltpu.VMEM((nc * PAIRS, CHUNK, 2 * CHUNK), BF16),
            pltpu.VMEM((nc * HEADS, HALO, HEAD_DIM), F32),
        ],
        compiler_params=pltpu.CompilerParams(
            dimension_semantics=("parallel", "arbitrary"),
            vmem_limit_bytes=VMEM_LIMIT),
    )(h3, conv_w, ab3, abt, alog_row, dtb_row, alog_pk, dtb_pk, h3, norm_w.reshape(1, HEAD_DIM))


SB_DEAD_LOG2 = -150.0
SB_Q_SCALE = HEAD_DIM ** -0.5 * math.log2(math.e)


def _sb_kernel(q_ref, k_ref, v_ref, o_ref, *, tq, nh):
    i = pl.program_id(2)
    row = lax.broadcasted_iota(jnp.int32, (tq, tq), 0)
    col = lax.broadcasted_iota(jnp.int32, (tq, tq), 1)
    later = jnp.where(row > col, 1.0, 0.0).astype(BF16)
    hs = range(nh)
    lanes = [slice(h * HEAD_DIM, (h + 1) * HEAD_DIM) for h in hs]
    qs = [q_ref[:, lanes[h]] for h in hs]

    def block(j, accs, runs, mask):
        k0 = pl.multiple_of(j * tq, tq)
        zs = [_dot_nt(qs[h], k_ref[pl.ds(k0, tq), lanes[h]]) for h in hs]
        keeps = []
        for z in zs:
            nz = -z
            keep = jnp.minimum(nz, 0.0) - jnp.log2(1.0 + jnp.exp2(jnp.minimum(z, nz)))
            keeps.append(keep if mask is None else jnp.where(mask, keep, 0.0))
        inner = [_dot(keep.astype(BF16), later) for keep in keeps]
        new_accs, new_runs = [], []
        for h in hs:
            a = jnp.exp2(zs[h] + keeps[h] + inner[h] + runs[h])
            if mask is not None:
                a = jnp.where(mask, a, 0.0)
            new_accs.append(accs[h] + _dot(a.astype(BF16), v_ref[pl.ds(k0, tq), lanes[h]]))
            new_runs.append(runs[h] + jnp.sum(keeps[h], axis=1, keepdims=True))
        return new_accs, new_runs

    accs = [jnp.zeros((tq, HEAD_DIM), F32) for _ in hs]
    runs = [jnp.zeros((tq, 1), F32) for _ in hs]
    accs, runs = block(i, accs, runs, col < row)

    def alive(runs):
        m = runs[0]
        for r in runs[1:]:
            m = jnp.maximum(m, r)
        return jnp.max(m) > SB_DEAD_LOG2

    def cond(carry):
        j, _, _, live = carry
        return jnp.logical_and(j >= 0, live)

    def body(carry):
        j, accs, runs, _ = carry
        accs, runs = block(j, list(accs), list(runs), None)
        return j - 1, tuple(accs), tuple(runs), alive(runs)

    _, accs, _, _ = lax.while_loop(cond, body, (i - 1, tuple(accs), tuple(runs), alive(runs)))
    for h in hs:
        o_ref[:, lanes[h]] = accs[h].astype(o_ref.dtype)


def _sb_attention(h3, tq, nh):
    b, t, _ = h3.shape
    kern = functools.partial(_sb_kernel, tq=tq, nh=nh)
    w = nh * HEAD_DIM
    return pl.pallas_call(
        kern,
        name="sb_attn",
        grid=(b, HEADS // nh, t // tq),
        in_specs=[
            pl.BlockSpec((None, tq, w), lambda bi, hi, qi: (bi, qi, COL_SB_Q // nh + hi)),
            pl.BlockSpec((None, t, w), lambda bi, hi, qi: (bi, 0, COL_SB_K // nh + hi)),
            pl.BlockSpec((None, t, w), lambda bi, hi, qi: (bi, 0, COL_SB_V // nh + hi)),
        ],
        out_specs=pl.BlockSpec((None, tq, w), lambda bi, hi, qi: (bi, qi, hi)),
        out_shape=jax.ShapeDtypeStruct((b, t, WIDTH), BF16),
        compiler_params=pltpu.CompilerParams(
            dimension_semantics=("parallel", "parallel", "arbitrary"),
            vmem_limit_bytes=VMEM_LIMIT),
    )(h3, h3, h3)


def _merge_kernel(x_ref, odn_ref, osb_ref, gdn_ref, gsb_ref, pa_ref, pb_ref, wo_ref, nw_ref,
                  x1_ref, n2_ref):
    pa = _dot(odn_ref[...], pa_ref[...])
    pb = _dot(osb_ref[...], pb_ref[...])
    mixed = (_sigmoid(gdn_ref[...].astype(F32)) * pa + _sigmoid(gsb_ref[...].astype(F32)) * pb)
    x1 = x_ref[...] + _dot(mixed.astype(BF16), wo_ref[...])
    x1_ref[...] = x1
    ms = jnp.mean(x1 * x1, axis=-1, keepdims=True)
    n2_ref[...] = (x1 * lax.rsqrt(ms + EPS) * nw_ref[...]).astype(n2_ref.dtype)


def _merge(x2, odn2, osb2, h2, p_dn, p_sb, w_out, norm_w, tm):
    m = x2.shape[0]
    gcol = COL_MERGE * 128 // D_MODEL
    row_spec = lambda c: pl.BlockSpec((tm, D_MODEL), lambda i: (i, c))
    w_spec = pl.BlockSpec((D_MODEL, D_MODEL), lambda i: (0, 0))
    return pl.pallas_call(
        _merge_kernel,
        name="merge",
        grid=(m // tm,),
        in_specs=[row_spec(0), row_spec(0), row_spec(0), row_spec(gcol), row_spec(gcol + 1),
                  w_spec, w_spec, w_spec, pl.BlockSpec((1, D_MODEL), lambda i: (0, 0))],
        out_specs=[row_spec(0), row_spec(0)],
        out_shape=[jax.ShapeDtypeStruct((m, D_MODEL), F32),
                   jax.ShapeDtypeStruct((m, D_MODEL), BF16)],
        compiler_params=pltpu.CompilerParams(
            dimension_semantics=("parallel",),
            vmem_limit_bytes=VMEM_LIMIT),
    )(x2, odn2, osb2, h2, h2, p_dn, p_sb, w_out, norm_w)


def _ffn_kernel(n2_ref, x1_ref, wup_ref, cw_ref, wd_ref, nw_ref, o_ref,
                ext_ref, carry_ref, *, tm, tf):
    ti = pl.program_id(1)
    n2 = n2_ref[...]

    def conv(cols, slot):
        u = _dot(n2, wup_ref[:, cols])
        ext = ext_ref.at[slot % 2]
        ext[0:HALO, :] = jnp.where(ti == 0, 0.0, carry_ref[slot])
        ext[HALO:HALO + tm, :] = u
        carry_ref[slot] = u[tm - HALO:, :]
        y = cw_ref[2:3, cols] * u
        y = y + cw_ref[1:2, cols] * ext[HALO - 1:HALO - 1 + tm, :]
        y = y + cw_ref[0:1, cols] * ext[HALO - 2:HALO - 2 + tm, :]
        return y

    x2 = x1_ref[...]
    for k in range(D_FF // tf):
        gate = conv(slice(k * tf, (k + 1) * tf), 2 * k)
        up = conv(slice(D_FF + k * tf, D_FF + (k + 1) * tf), 2 * k + 1)
        act = (gate * _sigmoid(gate) * up).astype(BF16)
        x2 = x2 + _dot(act, wd_ref[k * tf:(k + 1) * tf, :])
    ms = jnp.mean(x2 * x2, axis=-1, keepdims=True)
    o_ref[...] = x2 * lax.rsqrt(ms + EPS) * nw_ref[...]


def _ffn(n2, x1, w_up, conv_w, w_down, norm_w, tm, tf):
    b, t, _ = n2.shape
    nk = D_FF // tf
    kern = functools.partial(_ffn_kernel, tm=tm, tf=tf)
    resident = dict(pipeline_mode=pl.Buffered(1))
    return pl.pallas_call(
        kern,
        name="ffn",
        grid=(b, t // tm),
        in_specs=[
            pl.BlockSpec((None, tm, D_MODEL), lambda bi, ti: (bi, ti, 0)),
            pl.BlockSpec((None, tm, D_MODEL), lambda bi, ti: (bi, ti, 0)),
            pl.BlockSpec((D_MODEL, 2 * D_FF), lambda bi, ti: (0, 0), **resident),
            pl.BlockSpec((FFN_CONV, 2 * D_FF), lambda bi, ti: (0, 0), **resident),
            pl.BlockSpec((D_FF, D_MODEL), lambda bi, ti: (0, 0), **resident),
            pl.BlockSpec((1, D_MODEL), lambda bi, ti: (0, 0)),
        ],
        out_specs=pl.BlockSpec((None, tm, D_MODEL), lambda bi, ti: (bi, ti, 0)),
        out_shape=jax.ShapeDtypeStruct((b, t, D_MODEL), F32),
        scratch_shapes=[
            pltpu.VMEM((2, HALO + tm, tf), F32),
            pltpu.VMEM((2 * nk, HALO, tf), F32),
        ],
        compiler_params=pltpu.CompilerParams(
            dimension_semantics=("parallel", "arbitrary"),
            vmem_limit_bytes=VMEM_LIMIT),
    )(n2, x1, w_up, conv_w, w_down, norm_w)


def kernel(x, norm1_w, w_in, dn_conv_w, dn_A_log, dn_dt_bias, dn_norm_w, w_proj_dn, w_proj_sb,
           w_out, norm2_w, ffn_w_up, ffn_conv_w, ffn_w_down, norm_f_w):
    b, t, d = x.shape
    assert w_in.shape[0] == 1, "single-layer block: the final RMSNorm is fused into the FFN call"
    m = b * t
    tm = min(1024, t)
    tc = min(256, t)
    tq = min(256, t)
    qkv_end = 3 * WIDTH
    b_end = qkv_end + 2 * HEADS
    g_end = b_end + WIDTH
    wl = w_in[0]
    w_main = jnp.concatenate(
        [wl[:, :qkv_end], wl[:, b_end:g_end], wl[:, g_end:g_end + WIDTH] * SB_Q_SCALE,
         wl[:, g_end + WIDTH:]], axis=1).astype(BF16)
    w_ab = jnp.pad(wl[:, qkv_end:b_end], ((0, 0), (0, AB_LANES - 2 * HEADS))).astype(BF16)

    h2, ab2 = _in_proj(x.reshape(m, d), norm1_w[0].reshape(1, d), w_main, w_ab, tm, H_COLS // 3)
    h3 = h2.reshape(b, t, H_COLS)
    ab3 = ab2.reshape(b, t, AB_LANES)
    abt = (ab3[:, :, :2 * HEADS].reshape(b, t // CHUNK, CHUNK, 2 * HEADS).transpose(0, 1, 3, 2)
           .reshape(b, t // CHUNK, HEADS, 2 * CHUNK))

    o_dn = _gdn(h3, dn_conv_w[0], ab3, abt, dn_A_log[0], dn_dt_bias[0], dn_norm_w[0], tc)
    o_sb = _sb_attention(h3, tq, 4)

    x1, n2 = _merge(x.reshape(m, d), o_dn.reshape(m, WIDTH), o_sb.reshape(m, WIDTH), h2,
                    w_proj_dn[0].astype(BF16), w_proj_sb[0].astype(BF16),
                    w_out[0].astype(BF16), norm2_w[0].reshape(1, d), min(512, t))

    return _ffn(n2.reshape(b, t, d), x1.reshape(b, t, d), ffn_w_up[0].astype(BF16),
                ffn_conv_w[0], ffn_w_down[0].astype(BF16), norm_f_w.reshape(1, d),
                min(512, t), D_FF // 2)
```

```python
import functools
import math

import jax
import jax.numpy as jnp
from jax import lax
from jax.experimental import pallas as pl
from jax.experimental.pallas import tpu as pltpu

F32 = jnp.float32
BF16 = jnp.bfloat16

D_MODEL = 1024
HEADS = 8
HEAD_DIM = 128
WIDTH = HEADS * HEAD_DIM
DN_CONV = 4
CHUNK = 64
D_FF = 2816
FFN_CONV = 3
EPS = 1e-6

H_COLS = 3 * WIDTH + WIDTH + 3 * WIDTH + 2 * D_MODEL
COL_DN_GATE = 3 * WIDTH // 128
COL_SB_Q = COL_DN_GATE + WIDTH // 128
COL_SB_K = COL_SB_Q + HEADS
COL_SB_V = COL_SB_K + HEADS
COL_MERGE = COL_SB_V + HEADS
AB_LANES = 128
HALO = 8

VMEM_LIMIT = 56 * 1024 * 1024


def _dot(a, b):
    return jnp.dot(a, b, preferred_element_type=F32)


def _dot_nt(a, b):
    return lax.dot_general(a, b, (((1,), (1,)), ((), ())), preferred_element_type=F32)


def _dot_tn(a, b):
    return lax.dot_general(a, b, (((0,), (0,)), ((), ())), preferred_element_type=F32)


def _sigmoid(x):
    return 1.0 / (1.0 + jnp.exp(-x))


def _softplus(x):
    return jnp.maximum(x, 0.0) + jnp.log(1.0 + jnp.exp(-jnp.abs(x)))


CONV_GROUP = 512


def _in_proj_kernel(x_ref, nw_ref, w_ref, wab_ref, cw_ref, h_ref, ab_ref,
                    xn_ref, ext_ref, halo_ref, *, tm, tn, conv_steps, tiles_per_seq):
    i = pl.program_id(0)
    j = pl.program_id(1)

    @pl.when(j == 0)
    def _():
        x = x_ref[...]
        ms = jnp.mean(x * x, axis=-1, keepdims=True)
        xn_ref[...] = (x * lax.rsqrt(ms + EPS) * nw_ref[...]).astype(BF16)
        ab_ref[...] = _dot(xn_ref[...], wab_ref[...])

    @pl.when(j >= conv_steps)
    def _():
        h_ref[...] = _dot(xn_ref[...], w_ref[...]).astype(h_ref.dtype)

    @pl.when(j < conv_steps)
    def _():
        starts_sequence = (i % tiles_per_seq) == 0
        groups = tn // CONV_GROUP

        def project(g):
            ext_ref[g, HALO:HALO + tm, :] = _dot(xn_ref[...], w_ref[:, g * CONV_GROUP:(g + 1) * CONV_GROUP])

        project(0)
        for g in range(groups):
            if g + 1 < groups:
                project(g + 1)
            ext = ext_ref.at[g]
            for s in range(CONV_GROUP // HEAD_DIM):
                es = slice(s * HEAD_DIM, (s + 1) * HEAD_DIM)
                ls = slice(g * CONV_GROUP + s * HEAD_DIM, g * CONV_GROUP + (s + 1) * HEAD_DIM)
                ext[0:HALO, es] = jnp.where(starts_sequence, 0.0, halo_ref[j, :, ls])
                halo_ref[j, :, ls] = ext[tm:tm + HALO, es]
                y = cw_ref[DN_CONV - 1:DN_CONV, ls] * ext[HALO:HALO + tm, es]
                for tap in range(DN_CONV - 1):
                    shift = DN_CONV - 1 - tap
                    y = y + cw_ref[tap:tap + 1, ls] * ext[HALO - shift:HALO - shift + tm, es]
                h_ref[:, ls] = (y * _sigmoid(y)).astype(h_ref.dtype)


def _in_proj(x2, norm_w, w_main, w_ab, conv_w, tm, tn, seq_len):
    m = x2.shape[0]
    n = w_main.shape[1]
    conv_cols = conv_w.shape[1]
    assert conv_cols % tn == 0 and seq_len % tm == 0
    conv_steps = conv_cols // tn
    kern = functools.partial(_in_proj_kernel, tm=tm, tn=tn, conv_steps=conv_steps,
                             tiles_per_seq=seq_len // tm)
    return pl.pallas_call(
        kern,
        name="in_proj",
        grid=(m // tm, n // tn),
        in_specs=[
            pl.BlockSpec((tm, D_MODEL), lambda i, j: (i, 0)),
            pl.BlockSpec((1, D_MODEL), lambda i, j: (0, 0)),
            pl.BlockSpec((D_MODEL, tn), lambda i, j: (0, j)),
            pl.BlockSpec((D_MODEL, AB_LANES), lambda i, j: (0, 0)),
            pl.BlockSpec((DN_CONV, tn), lambda i, j: (0, jnp.minimum(j, conv_steps - 1))),
        ],
        out_specs=[
            pl.BlockSpec((tm, tn), lambda i, j: (i, j)),
            pl.BlockSpec((tm, AB_LANES), lambda i, j: (i, 0)),
        ],
        out_shape=[
            jax.ShapeDtypeStruct((m, n), BF16),
            jax.ShapeDtypeStruct((m, AB_LANES), F32),
        ],
        scratch_shapes=[
            pltpu.VMEM((tm, D_MODEL), BF16),
            pltpu.VMEM((tn // CONV_GROUP, HALO + tm, CONV_GROUP), F32),
            pltpu.VMEM((conv_steps, HALO, tn), F32),
        ],
        compiler_params=pltpu.CompilerParams(
            dimension_semantics=("arbitrary", "arbitrary"),
            vmem_limit_bytes=VMEM_LIMIT),
    )(x2, norm_w, w_main, w_ab, conv_w)


PAIRS = HEADS // 2
PREP_CHUNKS = 4


def _hi_lo(a):
    hi = a.astype(BF16)
    return hi, (a - hi.astype(F32)).astype(BF16)


def _lhs3(hi, lo):
    return jnp.concatenate([hi, hi, lo], axis=1)


def _rhs3(hi, lo):
    return jnp.concatenate([hi, lo, hi], axis=0)


def _gdn_kernel(act_ref, ab_ref, abt_ref, alog_row_ref, dtb_row_ref,
                alog_pk_ref, dtb_pk_ref, gate_ref, nw_ref, o_ref,
                s_ref, gcol_ref, bcol_ref, grow_ref,
                u_ref, w_ref, qg_ref, kd_ref, aqk_ref, dec_ref, *, tc):
    @pl.when(pl.program_id(1) == 0)
    def _():
        s_ref[...] = jnp.zeros_like(s_ref)

    ab = ab_ref[...]
    gcol_ref[...] = -jnp.exp(alog_row_ref[...]) * _softplus(ab + dtb_row_ref[...])
    bcol_ref[...] = _sigmoid(ab)
    grow_ref[...] = -jnp.exp(alog_pk_ref[...]) * _softplus(abt_ref[:, :PAIRS, :] + dtb_pk_ref[...])

    row = lax.broadcasted_iota(jnp.int32, (CHUNK, 2 * CHUNK), 0)
    lane = lax.broadcasted_iota(jnp.int32, (CHUNK, 2 * CHUNK), 1)
    col = lane & (CHUNK - 1)
    left = lane < CHUNK
    lower = col <= row
    strict = col < row
    eye = jnp.where(row == col, 1.0, 0.0).astype(F32)
    zero_bf = jnp.zeros((CHUNK, 2 * CHUNK), BF16)
    zero_tile = jnp.zeros((CHUNK, HEAD_DIM), BF16)

    def same_block(log2_size):
        return (row >> log2_size) == (col >> log2_size)

    def block_diag(x):
        return jnp.concatenate([jnp.where(left, x, zero_bf), jnp.where(left, zero_bf, x)], axis=0)

    def prepare(chunks):
        probs = []
        for c in chunks:
            rows = pl.ds(c * CHUNK if isinstance(c, int) else pl.multiple_of(c * CHUNK, CHUNK), CHUNK)
            g_cols = gcol_ref[rows, :]
            b_cols = bcol_ref[rows, :]
            g_rows = grow_ref[c]
            for pi in range(PAIRS):
                probs.append(dict(c=c, rows=rows, pi=pi, g_cols=g_cols, b_cols=b_cols,
                                  g_r=g_rows[pi:pi + 1, :]))
        for p in probs:
            rows, pi = p["rows"], p["pi"]
            hs = (2 * pi, 2 * pi + 1)
            masked = jnp.where(lower, p["g_r"], 0.0)
            gsum = [jnp.sum(masked[:, :CHUNK], axis=1, keepdims=True),
                    jnp.sum(masked[:, CHUNK:], axis=1, keepdims=True)]
            gc_c = jnp.where(left, gsum[0], gsum[1])
            g_c = jnp.where(left, p["g_cols"][:, hs[0]:hs[0] + 1], p["g_cols"][:, hs[1]:hs[1] + 1])
            gc_r = jnp.sum(jnp.where(row <= col, g_c, 0.0), axis=0, keepdims=True)
            decay = jnp.where(lower, jnp.exp(jnp.where(lower, gc_c - gc_r, 0.0)), 0.0)
            qs, ks, kbs, rhs_hi, rhs_lo = [], [], [], [], []
            for side, h in enumerate(hs):
                lanes = slice(h * HEAD_DIM, (h + 1) * HEAD_DIM)
                q = act_ref[rows, h * HEAD_DIM:(h + 1) * HEAD_DIM].astype(F32)
                k = act_ref[rows, WIDTH + h * HEAD_DIM:WIDTH + (h + 1) * HEAD_DIM].astype(F32)
                v = act_ref[rows, 2 * WIDTH + h * HEAD_DIM:2 * WIDTH + (h + 1) * HEAD_DIM].astype(F32)
                q = q * (lax.rsqrt(jnp.sum(q * q, axis=-1, keepdims=True) + EPS) * (HEAD_DIM ** -0.5))
                k = k * lax.rsqrt(jnp.sum(k * k, axis=-1, keepdims=True) + EPS)
                beta = p["b_cols"][:, HEADS + h:HEADS + h + 1]
                eg = jnp.exp(gsum[side])
                g_last = gsum[side][CHUNK - 1:CHUNK, :]
                kb = k * beta
                hi, lo = _hi_lo(jnp.concatenate([v * beta, kb * eg], axis=1))
                rhs_hi.append(hi)
                rhs_lo.append(lo)
                qs.append(q.astype(BF16))
                ks.append(k.astype(BF16))
                kbs.append(kb.astype(BF16))
                dec_ref[p["c"] * HEADS + h] = jnp.broadcast_to(jnp.exp(g_last), (HALO, HEAD_DIM))
                qg_ref[rows, lanes] = (q * eg).astype(BF16)
                kd_ref[rows, lanes] = (k * jnp.exp(g_last - gsum[side])).astype(BF16)
            p["rhs"] = _rhs3(jnp.concatenate(rhs_hi, axis=0), jnp.concatenate(rhs_lo, axis=0))
            kq = jnp.concatenate([jnp.concatenate(kbs, axis=1), jnp.concatenate(qs, axis=1)], axis=0)
            kt = jnp.concatenate([jnp.concatenate([ks[0], zero_tile], axis=1),
                                  jnp.concatenate([zero_tile, ks[1]], axis=1)], axis=0)
            kk_qk = _dot_nt(kq, kt)
            p["l"] = jnp.where(strict, kk_qk[:CHUNK] * decay, 0.0)
            aqk_ref[p["c"] * PAIRS + pi] = (kk_qk[CHUNK:] * decay).astype(BF16)
        invs = [eye - jnp.where(same_block(1), p["l"], 0.0) for p in probs]
        for lvl in range(1, 6):
            off_diag = same_block(lvl + 1) & jnp.logical_not(same_block(lvl))
            cs = [_hi_lo(jnp.where(off_diag, p["l"], 0.0)) for p in probs]
            ds = [_hi_lo(d) for d in invs]
            xs = [_hi_lo(_dot(_lhs3(*d), _rhs3(block_diag(c[0]), block_diag(c[1]))))
                  for d, c in zip(ds, cs)]
            invs = [inv - _dot(_lhs3(*x), _rhs3(block_diag(d[0]), block_diag(d[1])))
                    for inv, x, d in zip(invs, xs, ds)]
        for p, inv in zip(probs, invs):
            hi, lo = _hi_lo(inv)
            for side in range(2):
                h = 2 * p["pi"] + side
                lanes = slice(h * HEAD_DIM, (h + 1) * HEAD_DIM)
                keep = left if side == 0 else jnp.logical_not(left)
                uw = _dot(_lhs3(jnp.where(keep, hi, zero_bf), jnp.where(keep, lo, zero_bf)), p["rhs"])
                u_ref[p["rows"], lanes] = uw[:, :HEAD_DIM]
                w_ref[p["rows"], lanes] = uw[:, HEAD_DIM:].astype(BF16)

    def prepare_body(i, carry):
        prepare([PREP_CHUNKS * i + j for j in range(PREP_CHUNKS)])
        return carry

    if tc == PREP_CHUNKS * CHUNK:
        prepare(list(range(PREP_CHUNKS)))
    else:
        lax.fori_loop(0, tc // (PREP_CHUNKS * CHUNK), prepare_body, 0)

    nw = nw_ref[...]

    def recur_body(c, carry):
        rows = pl.ds(pl.multiple_of(c * CHUNK, CHUNK), CHUNK)
        hs = range(HEADS)
        lanes = [slice(h * HEAD_DIM, (h + 1) * HEAD_DIM) for h in hs]
        s_old = [s_ref[h] for h in hs]
        ws_qs = [_dot(jnp.concatenate([w_ref[rows, lanes[h]], qg_ref[rows, lanes[h]]], axis=0),
                      s_old[h].astype(BF16)) for h in hs]
        v_new = [(u_ref[rows, lanes[h]] - ws_qs[h][:CHUNK]).astype(BF16) for h in hs]
        for h in hs:
            s_ref[h] = (s_old[h] * dec_ref[c * HEADS + h][0:1, :]
                        + _dot_tn(kd_ref[rows, lanes[h]], v_new[h]))
        intra = []
        for pi in range(PAIRS):
            v_bd = jnp.concatenate([jnp.concatenate([v_new[2 * pi], zero_tile], axis=1),
                                    jnp.concatenate([zero_tile, v_new[2 * pi + 1]], axis=1)], axis=0)
            av = _dot(aqk_ref[c * PAIRS + pi], v_bd)
            intra += [av[:, :HEAD_DIM], av[:, HEAD_DIM:]]
        for h in hs:
            o = ws_qs[h][CHUNK:] + intra[h]
            o = o * lax.rsqrt(jnp.mean(o * o, axis=-1, keepdims=True) + EPS) * nw
            gate = gate_ref[rows, lanes[h]].astype(F32)
            o_ref[rows, lanes[h]] = (o * (gate * _sigmoid(gate))).astype(o_ref.dtype)
        return carry

    lax.fori_loop(0, tc // CHUNK, recur_body, 0)


def _gdn(h3, ab3, abt, alog, dtb, norm_w, tc):
    b, t, _ = h3.shape
    nc = tc // CHUNK
    assert nc % PREP_CHUNKS == 0
    alog_row = jnp.zeros((1, AB_LANES), F32).at[0, :HEADS].set(alog)
    dtb_row = jnp.zeros((1, AB_LANES), F32).at[0, :HEADS].set(dtb)
    alog_pk = jnp.repeat(alog, CHUNK).reshape(PAIRS, 2 * CHUNK)
    dtb_pk = jnp.repeat(dtb, CHUNK).reshape(PAIRS, 2 * CHUNK)
    kern = functools.partial(_gdn_kernel, tc=tc)
    return pl.pallas_call(
        kern,
        name="gdn",
        grid=(b, t // tc),
        in_specs=[
            pl.BlockSpec((None, tc, 3 * WIDTH), lambda bi, ti: (bi, ti, 0)),
            pl.BlockSpec((None, tc, AB_LANES), lambda bi, ti: (bi, ti, 0)),
            pl.BlockSpec((None, nc, HEADS, 2 * CHUNK), lambda bi, ti: (bi, ti, 0, 0)),
            pl.BlockSpec((1, AB_LANES), lambda bi, ti: (0, 0)),
            pl.BlockSpec((1, AB_LANES), lambda bi, ti: (0, 0)),
            pl.BlockSpec((PAIRS, 2 * CHUNK), lambda bi, ti: (0, 0)),
            pl.BlockSpec((PAIRS, 2 * CHUNK), lambda bi, ti: (0, 0)),
            pl.BlockSpec((None, tc, WIDTH), lambda bi, ti: (bi, ti, COL_DN_GATE * 128 // WIDTH)),
            pl.BlockSpec((1, HEAD_DIM), lambda bi, ti: (0, 0)),
        ],
        out_specs=pl.BlockSpec((None, tc, WIDTH), lambda bi, ti: (bi, ti, 0)),
        out_shape=jax.ShapeDtypeStruct((b, t, WIDTH), BF16),
        scratch_shapes=[
            pltpu.VMEM((HEADS, HEAD_DIM, HEAD_DIM), F32),
            pltpu.VMEM((tc, AB_LANES), F32),
            pltpu.VMEM((tc, AB_LANES), F32),
            pltpu.VMEM((nc, PAIRS, 2 * CHUNK), F32),
            pltpu.VMEM((tc, WIDTH), F32),
            pltpu.VMEM((tc, WIDTH), BF16),
            pltpu.VMEM((tc, WIDTH), BF16),
            pltpu.VMEM((tc, WIDTH), BF16),
            pltpu.VMEM((nc * PAIRS, CHUNK, 2 * CHUNK), BF16),
            pltpu.VMEM((nc * HEADS, HALO, HEAD_DIM), F32),
        ],
        compiler_params=pltpu.CompilerParams(
            dimension_semantics=("parallel", "arbitrary"),
            vmem_limit_bytes=VMEM_LIMIT),
    )(h3, ab3, abt, alog_row, dtb_row, alog_pk, dtb_pk, h3, norm_w.reshape(1, HEAD_DIM))


SB_DEAD_LOG2 = -150.0
SB_Q_SCALE = HEAD_DIM ** -0.5 * math.log2(math.e)


def _sb_kernel(q_ref, k_ref, v_ref, o_ref, *, tq, nh):
    i = pl.program_id(2)
    row = lax.broadcasted_iota(jnp.int32, (tq, tq), 0)
    col = lax.broadcasted_iota(jnp.int32, (tq, tq), 1)
    later = jnp.where(row > col, 1.0, 0.0).astype(BF16)
    hs = range(nh)
    lanes = [slice(h * HEAD_DIM, (h + 1) * HEAD_DIM) for h in hs]
    qs = [q_ref[:, lanes[h]] for h in hs]

    def block(j, accs, runs, mask):
        k0 = pl.multiple_of(j * tq, tq)
        zs = [_dot_nt(qs[h], k_ref[pl.ds(k0, tq), lanes[h]]) for h in hs]
        keeps = []
        for z in zs:
            nz = -z
            keep = jnp.minimum(nz, 0.0) - jnp.log2(1.0 + jnp.exp2(jnp.minimum(z, nz)))
            keeps.append(keep if mask is None else jnp.where(mask, keep, 0.0))
        inner = [_dot(keep.astype(BF16), later) for keep in keeps]
        new_accs, new_runs = [], []
        for h in hs:
            a = jnp.exp2(zs[h] + keeps[h] + inner[h] + runs[h])
            if mask is not None:
                a = jnp.where(mask, a, 0.0)
            new_accs.append(accs[h] + _dot(a.astype(BF16), v_ref[pl.ds(k0, tq), lanes[h]]))
            new_runs.append(runs[h] + jnp.sum(keeps[h], axis=1, keepdims=True))
        return new_accs, new_runs

    accs = [jnp.zeros((tq, HEAD_DIM), F32) for _ in hs]
    runs = [jnp.zeros((tq, 1), F32) for _ in hs]
    accs, runs = block(i, accs, runs, col < row)

    def alive(runs):
        m = runs[0]
        for r in runs[1:]:
            m = jnp.maximum(m, r)
        return jnp.max(m) > SB_DEAD_LOG2

    def cond(carry):
        j, _, _, live = carry
        return jnp.logical_and(j >= 0, live)

    def body(carry):
        j, accs, runs, _ = carry
        accs, runs = block(j, list(accs), list(runs), None)
        return j - 1, tuple(accs), tuple(runs), alive(runs)

    _, accs, _, _ = lax.while_loop(cond, body, (i - 1, tuple(accs), tuple(runs), alive(runs)))
    for h in hs:
        o_ref[:, lanes[h]] = accs[h].astype(o_ref.dtype)


def _sb_attention(h3, tq, nh):
    b, t, _ = h3.shape
    kern = functools.partial(_sb_kernel, tq=tq, nh=nh)
    w = nh * HEAD_DIM
    return pl.pallas_call(
        kern,
        name="sb_attn",
        grid=(b, HEADS // nh, t // tq),
        in_specs=[
            pl.BlockSpec((None, tq, w), lambda bi, hi, qi: (bi, qi, COL_SB_Q // nh + hi)),
            pl.BlockSpec((None, t, w), lambda bi, hi, qi: (bi, 0, COL_SB_K // nh + hi)),
            pl.BlockSpec((None, t, w), lambda bi, hi, qi: (bi, 0, COL_SB_V // nh + hi)),
        ],
        out_specs=pl.BlockSpec((None, tq, w), lambda bi, hi, qi: (bi, qi, hi)),
        out_shape=jax.ShapeDtypeStruct((b, t, WIDTH), BF16),
        compiler_params=pltpu.CompilerParams(
            dimension_semantics=("parallel", "parallel", "arbitrary"),
            vmem_limit_bytes=VMEM_LIMIT),
    )(h3, h3, h3)


def _merge_kernel(x_ref, odn_ref, osb_ref, gdn_ref, gsb_ref, pa_ref, pb_ref, wo_ref, nw_ref,
                  x1_ref, n2_ref):
    pa = _dot(odn_ref[...], pa_ref[...])
    pb = _dot(osb_ref[...], pb_ref[...])
    mixed = (_sigmoid(gdn_ref[...].astype(F32)) * pa + _sigmoid(gsb_ref[...].astype(F32)) * pb)
    x1 = x_ref[...] + _dot(mixed.astype(BF16), wo_ref[...])
    x1_ref[...] = x1
    ms = jnp.mean(x1 * x1, axis=-1, keepdims=True)
    n2_ref[...] = (x1 * lax.rsqrt(ms + EPS) * nw_ref[...]).astype(n2_ref.dtype)


def _merge(x2, odn2, osb2, h2, p_dn, p_sb, w_out, norm_w, tm):
    m = x2.shape[0]
    gcol = COL_MERGE * 128 // D_MODEL
    row_spec = lambda c: pl.BlockSpec((tm, D_MODEL), lambda i: (i, c))
    w_spec = pl.BlockSpec((D_MODEL, D_MODEL), lambda i: (0, 0))
    return pl.pallas_call(
        _merge_kernel,
        name="merge",
        grid=(m // tm,),
        in_specs=[row_spec(0), row_spec(0), row_spec(0), row_spec(gcol), row_spec(gcol + 1),
                  w_spec, w_spec, w_spec, pl.BlockSpec((1, D_MODEL), lambda i: (0, 0))],
        out_specs=[row_spec(0), row_spec(0)],
        out_shape=[jax.ShapeDtypeStruct((m, D_MODEL), F32),
                   jax.ShapeDtypeStruct((m, D_MODEL), BF16)],
        compiler_params=pltpu.CompilerParams(
            dimension_semantics=("parallel",),
            vmem_limit_bytes=VMEM_LIMIT),
    )(x2, odn2, osb2, h2, h2, p_dn, p_sb, w_out, norm_w)


def _ffn_kernel(n2_ref, x1_ref, wup_ref, cw_ref, wd_ref, nw_ref, o_ref,
                ext_ref, carry_ref, *, tm, tf):
    ti = pl.program_id(1)
    n2 = n2_ref[...]

    def conv(cols, slot):
        u = _dot(n2, wup_ref[:, cols])
        ext = ext_ref.at[slot % 2]
        ext[0:HALO, :] = jnp.where(ti == 0, 0.0, carry_ref[slot])
        ext[HALO:HALO + tm, :] = u
        carry_ref[slot] = u[tm - HALO:, :]
        y = cw_ref[2:3, cols] * u
        y = y + cw_ref[1:2, cols] * ext[HALO - 1:HALO - 1 + tm, :]
        y = y + cw_ref[0:1, cols] * ext[HALO - 2:HALO - 2 + tm, :]
        return y

    x2 = x1_ref[...]
    for k in range(D_FF // tf):
        gate = conv(slice(k * tf, (k + 1) * tf), 2 * k)
        up = conv(slice(D_FF + k * tf, D_FF + (k + 1) * tf), 2 * k + 1)
        act = (gate * _sigmoid(gate) * up).astype(BF16)
        x2 = x2 + _dot(act, wd_ref[k * tf:(k + 1) * tf, :])
    ms = jnp.mean(x2 * x2, axis=-1, keepdims=True)
    o_ref[...] = x2 * lax.rsqrt(ms + EPS) * nw_ref[...]


def _ffn(n2, x1, w_up, conv_w, w_down, norm_w, tm, tf):
    b, t, _ = n2.shape
    nk = D_FF // tf
    kern = functools.partial(_ffn_kernel, tm=tm, tf=tf)
    resident = dict(pipeline_mode=pl.Buffered(1))
    return pl.pallas_call(
        kern,
        name="ffn",
        grid=(b, t // tm),
        in_specs=[
            pl.BlockSpec((None, tm, D_MODEL), lambda bi, ti: (bi, ti, 0)),
            pl.BlockSpec((None, tm, D_MODEL), lambda bi, ti: (bi, ti, 0)),
            pl.BlockSpec((D_MODEL, 2 * D_FF), lambda bi, ti: (0, 0), **resident),
            pl.BlockSpec((FFN_CONV, 2 * D_FF), lambda bi, ti: (0, 0), **resident),
            pl.BlockSpec((D_FF, D_MODEL), lambda bi, ti: (0, 0), **resident),
            pl.BlockSpec((1, D_MODEL), lambda bi, ti: (0, 0)),
        ],
        out_specs=pl.BlockSpec((None, tm, D_MODEL), lambda bi, ti: (bi, ti, 0)),
        out_shape=jax.ShapeDtypeStruct((b, t, D_MODEL), F32),
        scratch_shapes=[
            pltpu.VMEM((2, HALO + tm, tf), F32),
            pltpu.VMEM((2 * nk, HALO, tf), F32),
        ],
        compiler_params=pltpu.CompilerParams(
            dimension_semantics=("parallel", "arbitrary"),
            vmem_limit_bytes=VMEM_LIMIT),
    )(n2, x1, w_up, conv_w, w_down, norm_w)


def kernel(x, norm1_w, w_in, dn_conv_w, dn_A_log, dn_dt_bias, dn_norm_w, w_proj_dn, w_proj_sb,
           w_out, norm2_w, ffn_w_up, ffn_conv_w, ffn_w_down, norm_f_w):
    b, t, d = x.shape
    assert w_in.shape[0] == 1, "single-layer block: the final RMSNorm is fused into the FFN call"
    m = b * t
    tm = min(1024, t)
    tc = min(256, t)
    tq = min(256, t)
    qkv_end = 3 * WIDTH
    b_end = qkv_end + 2 * HEADS
    g_end = b_end + WIDTH
    wl = w_in[0]
    w_main = jnp.concatenate(
        [wl[:, :qkv_end], wl[:, b_end:g_end], wl[:, g_end:g_end + WIDTH] * SB_Q_SCALE,
         wl[:, g_end + WIDTH:]], axis=1).astype(BF16)
    w_ab = jnp.pad(wl[:, qkv_end:b_end], ((0, 0), (0, AB_LANES - 2 * HEADS))).astype(BF16)

    h2, ab2 = _in_proj(x.reshape(m, d), norm1_w[0].reshape(1, d), w_main, w_ab, dn_conv_w[0],
                       tm, H_COLS // 6, t)
    h3 = h2.reshape(b, t, H_COLS)
    ab3 = ab2.reshape(b, t, AB_LANES)
    abt = (ab3[:, :, :2 * HEADS].reshape(b, t // CHUNK, CHUNK, 2 * HEADS).transpose(0, 1, 3, 2)
           .reshape(b, t // CHUNK, HEADS, 2 * CHUNK))

    o_dn = _gdn(h3, ab3, abt, dn_A_log[0], dn_dt_bias[0], dn_norm_w[0], tc)
    o_sb = _sb_attention(h3, tq, 4)

    x1, n2 = _merge(x.reshape(m, d), o_dn.reshape(m, WIDTH), o_sb.reshape(m, WIDTH), h2,
                    w_proj_dn[0].astype(BF16), w_proj_sb[0].astype(BF16),
                    w_out[0].astype(BF16), norm2_w[0].reshape(1, d), min(512, t))

    return _ffn(n2.reshape(b, t, d), x1.reshape(b, t, d), ffn_w_up[0].astype(BF16),
                ffn_conv_w[0], ffn_w_down[0].astype(BF16), norm_f_w.reshape(1, d),
                min(512, t), D_FF // 2)
```

```python
import functools
import math

import jax
import jax.numpy as jnp
from jax import lax
from jax.experimental import pallas as pl
from jax.experimental.pallas import tpu as pltpu

F32 = jnp.float32
BF16 = jnp.bfloat16

D_MODEL = 1024
HEADS = 8
HEAD_DIM = 128
WIDTH = HEADS * HEAD_DIM
DN_CONV = 4
CHUNK = 64
D_FF = 2816
FFN_CONV = 3
EPS = 1e-6

H_COLS = 3 * WIDTH + WIDTH + 3 * WIDTH + 2 * D_MODEL
COL_DN_GATE = 3 * WIDTH // 128
COL_SB_Q = COL_DN_GATE + WIDTH // 128
COL_SB_K = COL_SB_Q + HEADS
COL_SB_V = COL_SB_K + HEADS
COL_MERGE = COL_SB_V + HEADS
AB_LANES = 128
HALO = 8

VMEM_LIMIT = 56 * 1024 * 1024


def _dot(a, b):
    return jnp.dot(a, b, preferred_element_type=F32)


def _dot_nt(a, b):
    return lax.dot_general(a, b, (((1,), (1,)), ((), ())), preferred_element_type=F32)


def _dot_tn(a, b):
    return lax.dot_general(a, b, (((0,), (0,)), ((), ())), preferred_element_type=F32)


def _sigmoid(x):
    return 1.0 / (1.0 + jnp.exp(-x))


def _softplus(x):
    return jnp.maximum(x, 0.0) + jnp.log(1.0 + jnp.exp(-jnp.abs(x)))


def _in_proj_kernel(x_ref, nw_ref, w_ref, wab_ref, h_ref, ab_ref, xn_ref):
    @pl.when(pl.program_id(1) == 0)
    def _():
        x = x_ref[...]
        ms = jnp.mean(x * x, axis=-1, keepdims=True)
        xn_ref[...] = (x * lax.rsqrt(ms + EPS) * nw_ref[...]).astype(BF16)
        ab_ref[...] = _dot(xn_ref[...], wab_ref[...])

    h_ref[...] = _dot(xn_ref[...], w_ref[...]).astype(h_ref.dtype)


def _in_proj(x2, norm_w, w_main, w_ab, tm, tn):
    m = x2.shape[0]
    n = w_main.shape[1]
    return pl.pallas_call(
        _in_proj_kernel,
        name="in_proj",
        grid=(m // tm, n // tn),
        in_specs=[
            pl.BlockSpec((tm, D_MODEL), lambda i, j: (i, 0)),
            pl.BlockSpec((1, D_MODEL), lambda i, j: (0, 0)),
            pl.BlockSpec((D_MODEL, tn), lambda i, j: (0, j)),
            pl.BlockSpec((D_MODEL, AB_LANES), lambda i, j: (0, 0)),
        ],
        out_specs=[
            pl.BlockSpec((tm, tn), lambda i, j: (i, j)),
            pl.BlockSpec((tm, AB_LANES), lambda i, j: (i, 0)),
        ],
        out_shape=[
            jax.ShapeDtypeStruct((m, n), BF16),
            jax.ShapeDtypeStruct((m, AB_LANES), F32),
        ],
        scratch_shapes=[pltpu.VMEM((tm, D_MODEL), BF16)],
        compiler_params=pltpu.CompilerParams(
            dimension_semantics=("parallel", "arbitrary"),
            vmem_limit_bytes=VMEM_LIMIT),
    )(x2, norm_w, w_main, w_ab)


PAIRS = HEADS // 2
PREP_CHUNKS = 4


def _hi_lo(a):
    hi = a.astype(BF16)
    return hi, (a - hi.astype(F32)).astype(BF16)


def _lhs3(hi, lo):
    return jnp.concatenate([hi, hi, lo], axis=1)


def _rhs3(hi, lo):
    return jnp.concatenate([hi, lo, hi], axis=0)


def _gdn_kernel(qkv_ref, cw_ref, ab_ref, abt_ref, alog_row_ref, dtb_row_ref,
                alog_pk_ref, dtb_pk_ref, gate_ref, nw_ref, o_ref,
                s_ref, halo_ref, ext_ref, act_ref, gcol_ref, bcol_ref, grow_ref,
                u_ref, w_ref, qg_ref, kd_ref, aqk_ref, dec_ref, *, tc):
    @pl.when(pl.program_id(1) == 0)
    def _():
        s_ref[...] = jnp.zeros_like(s_ref)
        halo_ref[...] = jnp.zeros_like(halo_ref)

    for j in range(3 * WIDTH // HEAD_DIM):
        ls = slice(j * HEAD_DIM, (j + 1) * HEAD_DIM)
        x = qkv_ref[:, ls].astype(F32)
        ext_ref[0:HALO, ls] = halo_ref[:, ls]
        ext_ref[HALO:HALO + tc, ls] = x
        halo_ref[:, ls] = x[tc - HALO:, :]
        y = cw_ref[DN_CONV - 1:DN_CONV, ls] * x
        for tap in range(DN_CONV - 1):
            shift = DN_CONV - 1 - tap
            y = y + cw_ref[tap:tap + 1, ls] * ext_ref[HALO - shift:HALO - shift + tc, ls]
        act_ref[:, ls] = y * _sigmoid(y)

    ab = ab_ref[...]
    gcol_ref[...] = -jnp.exp(alog_row_ref[...]) * _softplus(ab + dtb_row_ref[...])
    bcol_ref[...] = _sigmoid(ab)
    grow_ref[...] = -jnp.exp(alog_pk_ref[...]) * _softplus(abt_ref[:, :PAIRS, :] + dtb_pk_ref[...])

    row = lax.broadcasted_iota(jnp.int32, (CHUNK, 2 * CHUNK), 0)
    lane = lax.broadcasted_iota(jnp.int32, (CHUNK, 2 * CHUNK), 1)
    col = lane & (CHUNK - 1)
    left = lane < CHUNK
    lower = col <= row
    strict = col < row
    eye = jnp.where(row == col, 1.0, 0.0).astype(F32)
    zero_bf = jnp.zeros((CHUNK, 2 * CHUNK), BF16)
    zero_tile = jnp.zeros((CHUNK, HEAD_DIM), BF16)

    def same_block(log2_size):
        return (row >> log2_size) == (col >> log2_size)

    def block_diag(x):
        return jnp.concatenate([jnp.where(left, x, zero_bf), jnp.where(left, zero_bf, x)], axis=0)

    def prepare(chunks):
        probs = []
        for c in chunks:
            rows = pl.ds(c * CHUNK if isinstance(c, int) else pl.multiple_of(c * CHUNK, CHUNK), CHUNK)
            g_cols = gcol_ref[rows, :]
            b_cols = bcol_ref[rows, :]
            g_rows = grow_ref[c]
            for pi in range(PAIRS):
                probs.append(dict(c=c, rows=rows, pi=pi, g_cols=g_cols, b_cols=b_cols,
                                  g_r=g_rows[pi:pi + 1, :]))
        for p in probs:
            rows, pi = p["rows"], p["pi"]
            hs = (2 * pi, 2 * pi + 1)
            masked = jnp.where(lower, p["g_r"], 0.0)
            gsum = [jnp.sum(masked[:, :CHUNK], axis=1, keepdims=True),
                    jnp.sum(masked[:, CHUNK:], axis=1, keepdims=True)]
            gc_c = jnp.where(left, gsum[0], gsum[1])
            g_c = jnp.where(left, p["g_cols"][:, hs[0]:hs[0] + 1], p["g_cols"][:, hs[1]:hs[1] + 1])
            gc_r = jnp.sum(jnp.where(row <= col, g_c, 0.0), axis=0, keepdims=True)
            decay = jnp.where(lower, jnp.exp(jnp.where(lower, gc_c - gc_r, 0.0)), 0.0)
            qs, ks, kbs, rhs = [], [], [], []
            for side, h in enumerate(hs):
                lanes = slice(h * HEAD_DIM, (h + 1) * HEAD_DIM)
                q = act_ref[rows, h * HEAD_DIM:(h + 1) * HEAD_DIM]
                k = act_ref[rows, WIDTH + h * HEAD_DIM:WIDTH + (h + 1) * HEAD_DIM]
                v = act_ref[rows, 2 * WIDTH + h * HEAD_DIM:2 * WIDTH + (h + 1) * HEAD_DIM]
                q = q * (lax.rsqrt(jnp.sum(q * q, axis=-1, keepdims=True) + EPS) * (HEAD_DIM ** -0.5))
                k = k * lax.rsqrt(jnp.sum(k * k, axis=-1, keepdims=True) + EPS)
                beta = p["b_cols"][:, HEADS + h:HEADS + h + 1]
                eg = jnp.exp(gsum[side])
                g_last = gsum[side][CHUNK - 1:CHUNK, :]
                kb = k * beta
                rhs.append(jnp.concatenate([v * beta, kb * eg], axis=1).astype(BF16))
                qs.append(q.astype(BF16))
                ks.append(k.astype(BF16))
                kbs.append(kb.astype(BF16))
                dec_ref[p["c"] * HEADS + h] = jnp.broadcast_to(jnp.exp(g_last), (HALO, HEAD_DIM))
                qg_ref[rows, lanes] = (q * eg).astype(BF16)
                kd_ref[rows, lanes] = (k * jnp.exp(g_last - gsum[side])).astype(BF16)
            p["rhs"] = jnp.concatenate(rhs + rhs, axis=0)
            kq = jnp.concatenate([jnp.concatenate(kbs, axis=1), jnp.concatenate(qs, axis=1)], axis=0)
            kt = jnp.concatenate([jnp.concatenate([ks[0], zero_tile], axis=1),
                                  jnp.concatenate([zero_tile, ks[1]], axis=1)], axis=0)
            kk_qk = _dot_nt(kq, kt)
            p["l"] = jnp.where(strict, kk_qk[:CHUNK] * decay, 0.0)
            aqk_ref[p["c"] * PAIRS + pi] = (kk_qk[CHUNK:] * decay).astype(BF16)
        invs = [eye - jnp.where(same_block(1), p["l"], 0.0) for p in probs]
        for lvl in range(1, 6):
            off_diag = same_block(lvl + 1) & jnp.logical_not(same_block(lvl))
            cs = [jnp.where(off_diag, p["l"], 0.0).astype(BF16) for p in probs]
            ds = [d.astype(BF16) for d in invs]
            xs = [_dot(d, block_diag(c)).astype(BF16) for d, c in zip(ds, cs)]
            invs = [inv - _dot(x, block_diag(d)) for inv, x, d in zip(invs, xs, ds)]
        t0s = [_hi_lo(t) for t in invs]
        ls = [_hi_lo(p["l"]) for p in probs]
        resid = [(eye - t) - _dot(_lhs3(*l), _rhs3(block_diag(t0[0]), block_diag(t0[1])))
                 for t, t0, l in zip(invs, t0s, ls)]
        invs = [t + _dot(t0[0], block_diag(r.astype(BF16))) for t, t0, r in zip(invs, t0s, resid)]
        for p, inv in zip(probs, invs):
            hi, lo = _hi_lo(inv)
            for side in range(2):
                h = 2 * p["pi"] + side
                lanes = slice(h * HEAD_DIM, (h + 1) * HEAD_DIM)
                keep = left if side == 0 else jnp.logical_not(left)
                t_hi_lo = jnp.concatenate([jnp.where(keep, hi, zero_bf), jnp.where(keep, lo, zero_bf)], axis=1)
                uw = _dot(t_hi_lo, p["rhs"])
                u_ref[p["rows"], lanes] = uw[:, :HEAD_DIM]
                w_ref[p["rows"], lanes] = uw[:, HEAD_DIM:].astype(BF16)

    def prepare_body(i, carry):
        prepare([PREP_CHUNKS * i + j for j in range(PREP_CHUNKS)])
        return carry

    if tc == PREP_CHUNKS * CHUNK:
        prepare(list(range(PREP_CHUNKS)))
    else:
        lax.fori_loop(0, tc // (PREP_CHUNKS * CHUNK), prepare_body, 0)

    nw = nw_ref[...]

    def recur_body(c, carry):
        rows = pl.ds(c * CHUNK, CHUNK)
        hs = range(HEADS)
        lanes = [slice(h * HEAD_DIM, (h + 1) * HEAD_DIM) for h in hs]
        s_old = [s_ref[h] for h in hs]
        ws_qs = [_dot(jnp.concatenate([w_ref[rows, lanes[h]], qg_ref[rows, lanes[h]]], axis=0),
                      s_old[h].astype(BF16)) for h in hs]
        v_new = [(u_ref[rows, lanes[h]] - ws_qs[h][:CHUNK]).astype(BF16) for h in hs]
        for h in hs:
            s_ref[h] = (s_old[h] * dec_ref[c * HEADS + h][0:1, :]
                        + _dot_tn(kd_ref[rows, lanes[h]], v_new[h]))
        intra = []
        for pi in range(PAIRS):
            v_bd = jnp.concatenate([jnp.concatenate([v_new[2 * pi], zero_tile], axis=1),
                                    jnp.concatenate([zero_tile, v_new[2 * pi + 1]], axis=1)], axis=0)
            av = _dot(aqk_ref[c * PAIRS + pi], v_bd)
            intra += [av[:, :HEAD_DIM], av[:, HEAD_DIM:]]
        for h in hs:
            o = ws_qs[h][CHUNK:] + intra[h]
            o = o * lax.rsqrt(jnp.mean(o * o, axis=-1, keepdims=True) + EPS) * nw
            gate = gate_ref[rows, lanes[h]].astype(F32)
            o_ref[rows, lanes[h]] = (o * (gate * _sigmoid(gate))).astype(o_ref.dtype)
        return carry

    for c in range(tc // CHUNK):
        recur_body(c, 0)


def _gdn(h3, conv_w, ab3, abt, alog, dtb, norm_w, tc):
    b, t, _ = h3.shape
    nc = tc // CHUNK
    assert nc % PREP_CHUNKS == 0
    alog_row = jnp.zeros((1, AB_LANES), F32).at[0, :HEADS].set(alog)
    dtb_row = jnp.zeros((1, AB_LANES), F32).at[0, :HEADS].set(dtb)
    alog_pk = jnp.repeat(alog, CHUNK).reshape(PAIRS, 2 * CHUNK)
    dtb_pk = jnp.repeat(dtb, CHUNK).reshape(PAIRS, 2 * CHUNK)
    kern = functools.partial(_gdn_kernel, tc=tc)
    return pl.pallas_call(
        kern,
        name="gdn",
        grid=(b, t // tc),
        in_specs=[
            pl.BlockSpec((None, tc, 3 * WIDTH), lambda bi, ti: (bi, ti, 0)),
            pl.BlockSpec((DN_CONV, 3 * WIDTH), lambda bi, ti: (0, 0)),
            pl.BlockSpec((None, tc, AB_LANES), lambda bi, ti: (bi, ti, 0)),
            pl.BlockSpec((None, nc, HEADS, 2 * CHUNK), lambda bi, ti: (bi, ti, 0, 0)),
            pl.BlockSpec((1, AB_LANES), lambda bi, ti: (0, 0)),
            pl.BlockSpec((1, AB_LANES), lambda bi, ti: (0, 0)),
            pl.BlockSpec((PAIRS, 2 * CHUNK), lambda bi, ti: (0, 0)),
            pl.BlockSpec((PAIRS, 2 * CHUNK), lambda bi, ti: (0, 0)),
            pl.BlockSpec((None, tc, WIDTH), lambda bi, ti: (bi, ti, COL_DN_GATE * 128 // WIDTH)),
            pl.BlockSpec((1, HEAD_DIM), lambda bi, ti: (0, 0)),
        ],
        out_specs=pl.BlockSpec((None, tc, WIDTH), lambda bi, ti: (bi, ti, 0)),
        out_shape=jax.ShapeDtypeStruct((b, t, WIDTH), BF16),
        scratch_shapes=[
            pltpu.VMEM((HEADS, HEAD_DIM, HEAD_DIM), F32),
            pltpu.VMEM((HALO, 3 * WIDTH), F32),
            pltpu.VMEM((HALO + tc, 3 * WIDTH), F32),
            pltpu.VMEM((tc, 3 * WIDTH), F32),
            pltpu.VMEM((tc, AB_LANES), F32),
            pltpu.VMEM((tc, AB_LANES), F32),
            pltpu.VMEM((nc, PAIRS, 2 * CHUNK), F32),
            pltpu.VMEM((tc, WIDTH), F32),
            pltpu.VMEM((tc, WIDTH), BF16),
            pltpu.VMEM((tc, WIDTH), BF16),
            pltpu.VMEM((tc, WIDTH), BF16),
            pltpu.VMEM((nc * PAIRS, CHUNK, 2 * CHUNK), BF16),
            pltpu.VMEM((nc * HEADS, HALO, HEAD_DIM), F32),
        ],
        compiler_params=pltpu.CompilerParams(
            dimension_semantics=("parallel", "arbitrary"),
            vmem_limit_bytes=VMEM_LIMIT),
    )(h3, conv_w, ab3, abt, alog_row, dtb_row, alog_pk, dtb_pk, h3, norm_w.reshape(1, HEAD_DIM))


SB_DEAD_LOG2 = -150.0
SB_Q_SCALE = HEAD_DIM ** -0.5 * math.log2(math.e)


def _sb_kernel(q_ref, k_ref, v_ref, o_ref, *, tq, nh):
    i = pl.program_id(2)
    row = lax.broadcasted_iota(jnp.int32, (tq, tq), 0)
    col = lax.broadcasted_iota(jnp.int32, (tq, tq), 1)
    later = jnp.where(row > col, 1.0, 0.0).astype(BF16)
    hs = range(nh)
    lanes = [slice(h * HEAD_DIM, (h + 1) * HEAD_DIM) for h in hs]
    qs = [q_ref[:, lanes[h]] for h in hs]

    def block(j, accs, runs, mask):
        k0 = pl.multiple_of(j * tq, tq)
        zs = [_dot_nt(qs[h], k_ref[pl.ds(k0, tq), lanes[h]]) for h in hs]
        keeps = []
        for z in zs:
            nz = -z
            keep = jnp.minimum(nz, 0.0) - jnp.log2(1.0 + jnp.exp2(jnp.minimum(z, nz)))
            keeps.append(keep if mask is None else jnp.where(mask, keep, 0.0))
        inner = [_dot(keep.astype(BF16), later) for keep in keeps]
        new_accs, new_runs = [], []
        for h in hs:
            a = jnp.exp2(zs[h] + keeps[h] + inner[h] + runs[h])
            if mask is not None:
                a = jnp.where(mask, a, 0.0)
            new_accs.append(accs[h] + _dot(a.astype(BF16), v_ref[pl.ds(k0, tq), lanes[h]]))
            new_runs.append(runs[h] + jnp.sum(keeps[h], axis=1, keepdims=True))
        return new_accs, new_runs

    accs = [jnp.zeros((tq, HEAD_DIM), F32) for _ in hs]
    runs = [jnp.zeros((tq, 1), F32) for _ in hs]
    accs, runs = block(i, accs, runs, col < row)

    def alive(runs):
        m = runs[0]
        for r in runs[1:]:
            m = jnp.maximum(m, r)
        return jnp.max(m) > SB_DEAD_LOG2

    def cond(carry):
        j, _, _, live = carry
        return jnp.logical_and(j >= 0, live)

    def body(carry):
        j, accs, runs, _ = carry
        accs, runs = block(j, list(accs), list(runs), None)
        return j - 1, tuple(accs), tuple(runs), alive(runs)

    _, accs, _, _ = lax.while_loop(cond, body, (i - 1, tuple(accs), tuple(runs), alive(runs)))
    for h in hs:
        o_ref[:, lanes[h]] = accs[h].astype(o_ref.dtype)


def _sb_attention(h3, tq, nh):
    b, t, _ = h3.shape
    kern = functools.partial(_sb_kernel, tq=tq, nh=nh)
    w = nh * HEAD_DIM
    return pl.pallas_call(
        kern,
        name="sb_attn",
        grid=(b, HEADS // nh, t // tq),
        in_specs=[
            pl.BlockSpec((None, tq, w), lambda bi, hi, qi: (bi, qi, COL_SB_Q // nh + hi)),
            pl.BlockSpec((None, t, w), lambda bi, hi, qi: (bi, 0, COL_SB_K // nh + hi)),
            pl.BlockSpec((None, t, w), lambda bi, hi, qi: (bi, 0, COL_SB_V // nh + hi)),
        ],
        out_specs=pl.BlockSpec((None, tq, w), lambda bi, hi, qi: (bi, qi, hi)),
        out_shape=jax.ShapeDtypeStruct((b, t, WIDTH), BF16),
        compiler_params=pltpu.CompilerParams(
            dimension_semantics=("parallel", "parallel", "arbitrary"),
            vmem_limit_bytes=VMEM_LIMIT),
    )(h3, h3, h3)


def _merge_kernel(x_ref, odn_ref, osb_ref, gdn_ref, gsb_ref, pa_ref, pb_ref, wo_ref, nw_ref,
                  x1_ref, n2_ref):
    pa = _dot(odn_ref[...], pa_ref[...])
    pb = _dot(osb_ref[...], pb_ref[...])
    mixed = (_sigmoid(gdn_ref[...].astype(F32)) * pa + _sigmoid(gsb_ref[...].astype(F32)) * pb)
    x1 = x_ref[...] + _dot(mixed.astype(BF16), wo_ref[...])
    x1_ref[...] = x1
    ms = jnp.mean(x1 * x1, axis=-1, keepdims=True)
    n2_ref[...] = (x1 * lax.rsqrt(ms + EPS) * nw_ref[...]).astype(n2_ref.dtype)


def _merge(x2, odn2, osb2, h2, p_dn, p_sb, w_out, norm_w, tm):
    m = x2.shape[0]
    gcol = COL_MERGE * 128 // D_MODEL
    row_spec = lambda c: pl.BlockSpec((tm, D_MODEL), lambda i: (i, c))
    w_spec = pl.BlockSpec((D_MODEL, D_MODEL), lambda i: (0, 0))
    return pl.pallas_call(
        _merge_kernel,
        name="merge",
        grid=(m // tm,),
        in_specs=[row_spec(0), row_spec(0), row_spec(0), row_spec(gcol), row_spec(gcol + 1),
                  w_spec, w_spec, w_spec, pl.BlockSpec((1, D_MODEL), lambda i: (0, 0))],
        out_specs=[row_spec(0), row_spec(0)],
        out_shape=[jax.ShapeDtypeStruct((m, D_MODEL), F32),
                   jax.ShapeDtypeStruct((m, D_MODEL), BF16)],
        compiler_params=pltpu.CompilerParams(
            dimension_semantics=("parallel",),
            vmem_limit_bytes=VMEM_LIMIT),
    )(x2, odn2, osb2, h2, h2, p_dn, p_sb, w_out, norm_w)


def _ffn_kernel(n2_ref, x1_ref, wup_ref, cw_ref, wd_ref, nw_ref, o_ref,
                ext_ref, carry_ref, *, tm, tf):
    ti = pl.program_id(1)
    n2 = n2_ref[...]

    def conv(cols, slot):
        u = _dot(n2, wup_ref[:, cols])
        ext = ext_ref.at[slot % 2]
        ext[0:HALO, :] = jnp.where(ti == 0, 0.0, carry_ref[slot])
        ext[HALO:HALO + tm, :] = u
        carry_ref[slot] = u[tm - HALO:, :]
        y = cw_ref[2:3, cols] * u
        y = y + cw_ref[1:2, cols] * ext[HALO - 1:HALO - 1 + tm, :]
        y = y + cw_ref[0:1, cols] * ext[HALO - 2:HALO - 2 + tm, :]
        return y

    x2 = x1_ref[...]
    for k in range(D_FF // tf):
        gate = conv(slice(k * tf, (k + 1) * tf), 2 * k)
        up = conv(slice(D_FF + k * tf, D_FF + (k + 1) * tf), 2 * k + 1)
        act = (gate * _sigmoid(gate) * up).astype(BF16)
        x2 = x2 + _dot(act, wd_ref[k * tf:(k + 1) * tf, :])
    ms = jnp.mean(x2 * x2, axis=-1, keepdims=True)
    o_ref[...] = x2 * lax.rsqrt(ms + EPS) * nw_ref[...]


def _ffn(n2, x1, w_up, conv_w, w_down, norm_w, tm, tf):
    b, t, _ = n2.shape
    nk = D_FF // tf
    kern = functools.partial(_ffn_kernel, tm=tm, tf=tf)
    resident = dict(pipeline_mode=pl.Buffered(1))
    return pl.pallas_call(
        kern,
        name="ffn",
        grid=(b, t // tm),
        in_specs=[
            pl.BlockSpec((None, tm, D_MODEL), lambda bi, ti: (bi, ti, 0)),
            pl.BlockSpec((None, tm, D_MODEL), lambda bi, ti: (bi, ti, 0)),
            pl.BlockSpec((D_MODEL, 2 * D_FF), lambda bi, ti: (0, 0), **resident),
            pl.BlockSpec((FFN_CONV, 2 * D_FF), lambda bi, ti: (0, 0), **resident),
            pl.BlockSpec((D_FF, D_MODEL), lambda bi, ti: (0, 0), **resident),
            pl.BlockSpec((1, D_MODEL), lambda bi, ti: (0, 0)),
        ],
        out_specs=pl.BlockSpec((None, tm, D_MODEL), lambda bi, ti: (bi, ti, 0)),
        out_shape=jax.ShapeDtypeStruct((b, t, D_MODEL), F32),
        scratch_shapes=[
            pltpu.VMEM((2, HALO + tm, tf), F32),
            pltpu.VMEM((2 * nk, HALO, tf), F32),
        ],
        compiler_params=pltpu.CompilerParams(
            dimension_semantics=("parallel", "arbitrary"),
            vmem_limit_bytes=VMEM_LIMIT),
    )(n2, x1, w_up, conv_w, w_down, norm_w)


def kernel(x, norm1_w, w_in, dn_conv_w, dn_A_log, dn_dt_bias, dn_norm_w, w_proj_dn, w_proj_sb,
           w_out, norm2_w, ffn_w_up, ffn_conv_w, ffn_w_down, norm_f_w):
    b, t, d = x.shape
    assert w_in.shape[0] == 1, "single-layer block: the final RMSNorm is fused into the FFN call"
    m = b * t
    tm = min(1024, t)
    tc = min(256, t)
    tq = min(256, t)
    qkv_end = 3 * WIDTH
    b_end = qkv_end + 2 * HEADS
    g_end = b_end + WIDTH
    wl = w_in[0]
    w_main = jnp.concatenate(
        [wl[:, :qkv_end], wl[:, b_end:g_end], wl[:, g_end:g_end + WIDTH] * SB_Q_SCALE,
         wl[:, g_end + WIDTH:]], axis=1).astype(BF16)
    w_ab = jnp.pad(wl[:, qkv_end:b_end], ((0, 0), (0, AB_LANES - 2 * HEADS))).astype(BF16)

    h2, ab2 = _in_proj(x.reshape(m, d), norm1_w[0].reshape(1, d), w_main, w_ab, tm, H_COLS // 3)
    h3 = h2.reshape(b, t, H_COLS)
    ab3 = ab2.reshape(b, t, AB_LANES)
    abt = (ab3[:, :, :2 * HEADS].reshape(b, t // CHUNK, CHUNK, 2 * HEADS).transpose(0, 1, 3, 2)
           .reshape(b, t // CHUNK, HEADS, 2 * CHUNK))

    o_dn = _gdn(h3, dn_conv_w[0], ab3, abt, dn_A_log[0], dn_dt_bias[0], dn_norm_w[0], tc)
    o_sb = _sb_attention(h3, tq, 4)

    x1, n2 = _merge(x.reshape(m, d), o_dn.reshape(m, WIDTH), o_sb.reshape(m, WIDTH), h2,
                    w_proj_dn[0].astype(BF16), w_proj_sb[0].astype(BF16),
                    w_out[0].astype(BF16), norm2_w[0].reshape(1, d), min(512, t))

    return _ffn(n2.reshape(b, t, d), x1.reshape(b, t, d), ffn_w_up[0].astype(BF16),
                ffn_conv_w[0], ffn_w_down[0].astype(BF16), norm_f_w.reshape(1, d),
                min(512, t), D_FF // 2)
```

```python
import functools
import math

import numpy as np

import jax
import jax.numpy as jnp
from jax import lax
from jax.experimental import pallas as pl
from jax.experimental.pallas import tpu as pltpu

F32 = jnp.float32
BF16 = jnp.bfloat16

D_MODEL = 1024
HEADS = 8
HEAD_DIM = 128
WIDTH = HEADS * HEAD_DIM
DN_CONV = 4
CHUNK = 64
D_FF = 2816
FFN_CONV = 3
EPS = 1e-6

H_COLS = 3 * WIDTH + WIDTH + 3 * WIDTH + 2 * D_MODEL
COL_DN_GATE = 3 * WIDTH // 128
COL_SB_Q = COL_DN_GATE + WIDTH // 128
COL_SB_K = COL_SB_Q + HEADS
COL_SB_V = COL_SB_K + HEADS
COL_MERGE = COL_SB_V + HEADS
AB_LANES = 128
HALO = 8

VMEM_LIMIT = 56 * 1024 * 1024


def _dot(a, b):
    return jnp.dot(a, b, preferred_element_type=F32)


def _dot_nt(a, b):
    return lax.dot_general(a, b, (((1,), (1,)), ((), ())), preferred_element_type=F32)


def _dot_tn(a, b):
    return lax.dot_general(a, b, (((0,), (0,)), ((), ())), preferred_element_type=F32)


def _sigmoid(x):
    return 1.0 / (1.0 + jnp.exp(-x))


def _softplus(x):
    return jnp.maximum(x, 0.0) + jnp.log(1.0 + jnp.exp(-jnp.abs(x)))


def _in_proj_kernel(x_ref, nw_ref, w_ref, wab_ref, h_ref, ab_ref, xn_ref):
    @pl.when(pl.program_id(1) == 0)
    def _():
        x = x_ref[...]
        ms = jnp.mean(x * x, axis=-1, keepdims=True)
        xn_ref[...] = (x * lax.rsqrt(ms + EPS) * nw_ref[...]).astype(BF16)
        ab_ref[...] = _dot(xn_ref[...], wab_ref[...])

    h_ref[...] = _dot(xn_ref[...], w_ref[...]).astype(h_ref.dtype)


def _in_proj(x2, norm_w, w_main, w_ab, tm, tn):
    m = x2.shape[0]
    n = w_main.shape[1]
    return pl.pallas_call(
        _in_proj_kernel,
        name="in_proj",
        grid=(m // tm, n // tn),
        in_specs=[
            pl.BlockSpec((tm, D_MODEL), lambda i, j: (i, 0)),
            pl.BlockSpec((1, D_MODEL), lambda i, j: (0, 0)),
            pl.BlockSpec((D_MODEL, tn), lambda i, j: (0, j)),
            pl.BlockSpec((D_MODEL, AB_LANES), lambda i, j: (0, 0)),
        ],
        out_specs=[
            pl.BlockSpec((tm, tn), lambda i, j: (i, j)),
            pl.BlockSpec((tm, AB_LANES), lambda i, j: (i, 0)),
        ],
        out_shape=[
            jax.ShapeDtypeStruct((m, n), BF16),
            jax.ShapeDtypeStruct((m, AB_LANES), F32),
        ],
        scratch_shapes=[pltpu.VMEM((tm, D_MODEL), BF16)],
        compiler_params=pltpu.CompilerParams(
            dimension_semantics=("parallel", "arbitrary"),
            vmem_limit_bytes=VMEM_LIMIT),
    )(x2, norm_w, w_main, w_ab)


PAIRS = HEADS // 2
PREP_CHUNKS = 4


def _hi_lo(a):
    hi = a.astype(BF16)
    return hi, (a - hi.astype(F32)).astype(BF16)


def _lhs3(hi, lo):
    return jnp.concatenate([hi, hi, lo], axis=1)


def _rhs3(hi, lo):
    return jnp.concatenate([hi, lo, hi], axis=0)


def _gdn_kernel(qkv_ref, cw_ref, ab_ref, abt_ref, alog_row_ref, dtb_row_ref,
                alog_pk_ref, dtb_pk_ref, sel_beta_ref, sel_g_ref, sel_cum_ref, ones_ref,
                gate_ref, nw_ref, o_ref,
                s_ref, halo_ref, ext_ref, act_ref, gcol_ref, bcol_ref, grow_ref,
                u_ref, w_ref, qg_ref, kd_ref, aqk_ref, dec_ref, l_ref, rhs_ref, *, tc):
    @pl.when(pl.program_id(1) == 0)
    def _():
        s_ref[...] = jnp.zeros_like(s_ref)
        halo_ref[...] = jnp.zeros_like(halo_ref)

    for j in range(3 * WIDTH // HEAD_DIM):
        ls = slice(j * HEAD_DIM, (j + 1) * HEAD_DIM)
        x = qkv_ref[:, ls].astype(F32)
        ext_ref[0:HALO, ls] = halo_ref[:, ls]
        ext_ref[HALO:HALO + tc, ls] = x
        halo_ref[:, ls] = x[tc - HALO:, :]
        y = cw_ref[DN_CONV - 1:DN_CONV, ls] * x
        for tap in range(DN_CONV - 1):
            shift = DN_CONV - 1 - tap
            y = y + cw_ref[tap:tap + 1, ls] * ext_ref[HALO - shift:HALO - shift + tc, ls]
        act_ref[:, ls] = y * _sigmoid(y)

    ab = ab_ref[...]
    gcol_ref[...] = -jnp.exp(alog_row_ref[...]) * _softplus(ab + dtb_row_ref[...])
    bcol_ref[...] = _sigmoid(ab)
    grow_ref[...] = -jnp.exp(alog_pk_ref[...]) * _softplus(abt_ref[:, :PAIRS, :] + dtb_pk_ref[...])

    row = lax.broadcasted_iota(jnp.int32, (CHUNK, 2 * CHUNK), 0)
    lane = lax.broadcasted_iota(jnp.int32, (CHUNK, 2 * CHUNK), 1)
    col = lane & (CHUNK - 1)
    left = lane < CHUNK
    lower = col <= row
    strict = col < row
    eye = jnp.where(row == col, 1.0, 0.0).astype(F32)
    zero_bf = jnp.zeros((CHUNK, 2 * CHUNK), BF16)
    zero_tile = jnp.zeros((CHUNK, HEAD_DIM), BF16)

    def same_block(log2_size):
        return (row >> log2_size) == (col >> log2_size)

    def block_diag(x):
        return jnp.concatenate([jnp.where(left, x, zero_bf), jnp.where(left, zero_bf, x)], axis=0)

    def chunk_inputs(c, carry):
        rows = pl.ds(c * CHUNK, CHUNK)
        beta_all = _dot(jnp.concatenate(_hi_lo(bcol_ref[rows, :]), axis=1), sel_beta_ref[...])
        g_pairs = _dot(jnp.concatenate(_hi_lo(gcol_ref[rows, :]), axis=1), sel_g_ref[...])
        g_rows = grow_ref[c]
        probs = [dict(c=c, rows=rows, pi=pi, beta_all=beta_all,
                      g_c=g_pairs[:, pi * 2 * CHUNK:(pi + 1) * 2 * CHUNK],
                      g_r=g_rows[pi:pi + 1, :]) for pi in range(PAIRS)]
        for p in probs:
            rows, pi = p["rows"], p["pi"]
            hs = (2 * pi, 2 * pi + 1)
            masked = jnp.where(lower, p["g_r"], 0.0)
            cum = _dot(jnp.concatenate(_hi_lo(masked), axis=1), sel_cum_ref[...])
            gc_c = cum[:, :2 * CHUNK]
            gfull = [cum[:, HEAD_DIM:2 * HEAD_DIM], cum[:, 2 * HEAD_DIM:]]
            gc_r = jnp.sum(jnp.where(row <= col, p["g_c"], 0.0), axis=0, keepdims=True)
            decay = jnp.where(lower, jnp.exp(jnp.where(lower, gc_c - gc_r, 0.0)), 0.0)
            qs, ks, kbs, rhs = [], [], [], []
            for side, h in enumerate(hs):
                lanes = slice(h * HEAD_DIM, (h + 1) * HEAD_DIM)
                q = act_ref[rows, h * HEAD_DIM:(h + 1) * HEAD_DIM]
                k = act_ref[rows, WIDTH + h * HEAD_DIM:WIDTH + (h + 1) * HEAD_DIM]
                v = act_ref[rows, 2 * WIDTH + h * HEAD_DIM:2 * WIDTH + (h + 1) * HEAD_DIM]
                sumsq = _dot(jnp.concatenate([q * q, k * k], axis=0).astype(BF16), ones_ref[...])
                q = q * (lax.rsqrt(sumsq[:CHUNK] + EPS) * (HEAD_DIM ** -0.5))
                k = k * lax.rsqrt(sumsq[CHUNK:] + EPS)
                beta = p["beta_all"][:, lanes]
                eg = jnp.exp(gfull[side])
                g_last = gfull[side][CHUNK - 1:CHUNK, :]
                kb = k * beta
                rhs.append(jnp.concatenate([v * beta, kb * eg], axis=1).astype(BF16))
                qs.append(q.astype(BF16))
                ks.append(k.astype(BF16))
                kbs.append(kb.astype(BF16))
                dec_ref[p["c"] * HEADS + h] = jnp.broadcast_to(jnp.exp(g_last), (HALO, HEAD_DIM))
                qg_ref[rows, lanes] = (q * eg).astype(BF16)
                kd_ref[rows, lanes] = (k * jnp.exp(g_last - gfull[side])).astype(BF16)
            rhs_ref[p["c"] * PAIRS + pi] = jnp.concatenate(rhs, axis=0)
            kq = jnp.concatenate([jnp.concatenate(kbs, axis=1), jnp.concatenate(qs, axis=1)], axis=0)
            kt = jnp.concatenate([jnp.concatenate([ks[0], zero_tile], axis=1),
                                  jnp.concatenate([zero_tile, ks[1]], axis=1)], axis=0)
            kk_qk = _dot_nt(kq, kt)
            l_ref[p["c"] * PAIRS + pi] = jnp.where(strict, kk_qk[:CHUNK] * decay, 0.0)
            aqk_ref[p["c"] * PAIRS + pi] = (kk_qk[CHUNK:] * decay).astype(BF16)
        return carry

    def solve(chunks):
        probs = [dict(c=c, pi=pi, rows=pl.ds(c * CHUNK, CHUNK), l=l_ref[c * PAIRS + pi])
                 for c in chunks for pi in range(PAIRS)]
        invs = [eye - jnp.where(same_block(1), p["l"], 0.0) for p in probs]
        for lvl in range(1, 6):
            off_diag = same_block(lvl + 1) & jnp.logical_not(same_block(lvl))
            cs = [jnp.where(off_diag, p["l"], 0.0).astype(BF16) for p in probs]
            ds = [d.astype(BF16) for d in invs]
            xs = [_dot(d, block_diag(c)).astype(BF16) for d, c in zip(ds, cs)]
            invs = [inv - _dot(x, block_diag(d)) for inv, x, d in zip(invs, xs, ds)]
        t0s = [_hi_lo(t) for t in invs]
        ls = [_hi_lo(p["l"]) for p in probs]
        resid = [(eye - t) - _dot(_lhs3(*l), _rhs3(block_diag(t0[0]), block_diag(t0[1])))
                 for t, t0, l in zip(invs, t0s, ls)]
        invs = [t + _dot(t0[0], block_diag(r.astype(BF16))) for t, t0, r in zip(invs, t0s, resid)]
        for p, inv in zip(probs, invs):
            hi, lo = _hi_lo(inv)
            rhs = rhs_ref[p["c"] * PAIRS + p["pi"]]
            rhs = jnp.concatenate([rhs, rhs], axis=0)
            for side in range(2):
                h = 2 * p["pi"] + side
                lanes = slice(h * HEAD_DIM, (h + 1) * HEAD_DIM)
                keep = left if side == 0 else jnp.logical_not(left)
                t_hi_lo = jnp.concatenate([jnp.where(keep, hi, zero_bf), jnp.where(keep, lo, zero_bf)], axis=1)
                uw = _dot(t_hi_lo, rhs)
                u_ref[p["rows"], lanes] = uw[:, :HEAD_DIM]
                w_ref[p["rows"], lanes] = uw[:, HEAD_DIM:].astype(BF16)

    for c in range(tc // CHUNK):
        chunk_inputs(c, 0)
    for i in range(tc // (PREP_CHUNKS * CHUNK)):
        solve([PREP_CHUNKS * i + j for j in range(PREP_CHUNKS)])

    nw = nw_ref[...]

    def recur_body(c, carry):
        rows = pl.ds(c * CHUNK, CHUNK)
        hs = range(HEADS)
        lanes = [slice(h * HEAD_DIM, (h + 1) * HEAD_DIM) for h in hs]
        s_old = [s_ref[h] for h in hs]
        ws_qs = [_dot(jnp.concatenate([w_ref[rows, lanes[h]], qg_ref[rows, lanes[h]]], axis=0),
                      s_old[h].astype(BF16)) for h in hs]
        v_new = [(u_ref[rows, lanes[h]] - ws_qs[h][:CHUNK]).astype(BF16) for h in hs]
        for h in hs:
            s_ref[h] = (s_old[h] * dec_ref[c * HEADS + h][0:1, :]
                        + _dot_tn(kd_ref[rows, lanes[h]], v_new[h]))
        intra = []
        for pi in range(PAIRS):
            v_bd = jnp.concatenate([jnp.concatenate([v_new[2 * pi], zero_tile], axis=1),
                                    jnp.concatenate([zero_tile, v_new[2 * pi + 1]], axis=1)], axis=0)
            av = _dot(aqk_ref[c * PAIRS + pi], v_bd)
            intra += [av[:, :HEAD_DIM], av[:, HEAD_DIM:]]
        for h in hs:
            o = ws_qs[h][CHUNK:] + intra[h]
            o = o * lax.rsqrt(jnp.mean(o * o, axis=-1, keepdims=True) + EPS) * nw
            gate = gate_ref[rows, lanes[h]].astype(F32)
            o_ref[rows, lanes[h]] = (o * (gate * _sigmoid(gate))).astype(o_ref.dtype)
        return carry

    for c in range(tc // CHUNK):
        recur_body(c, 0)


def _gdn_selectors():
    src = np.arange(2 * AB_LANES)[:, None] % AB_LANES
    out = np.arange(HEADS * HEAD_DIM)[None, :]
    sel_beta = src == HEADS + out // HEAD_DIM
    out_g = out[:, :PAIRS * 2 * CHUNK]
    sel_g = src == 2 * (out_g // (2 * CHUNK)) + (out_g % (2 * CHUNK)) // CHUNK
    out_c = np.arange(3 * HEAD_DIM)[None, :]
    src_left = src < CHUNK
    sel_cum = np.where(out_c < HEAD_DIM, src_left == (out_c < CHUNK),
                       np.where(out_c < 2 * HEAD_DIM, src_left, ~src_left))
    ones = np.ones((HEAD_DIM, HEAD_DIM))
    return tuple(jnp.asarray(a, BF16) for a in (sel_beta, sel_g, sel_cum, ones))


def _gdn(h3, conv_w, ab3, abt, alog, dtb, norm_w, tc):
    b, t, _ = h3.shape
    nc = tc // CHUNK
    assert nc % PREP_CHUNKS == 0
    alog_row = jnp.zeros((1, AB_LANES), F32).at[0, :HEADS].set(alog)
    dtb_row = jnp.zeros((1, AB_LANES), F32).at[0, :HEADS].set(dtb)
    alog_pk = jnp.repeat(alog, CHUNK).reshape(PAIRS, 2 * CHUNK)
    dtb_pk = jnp.repeat(dtb, CHUNK).reshape(PAIRS, 2 * CHUNK)
    sel_beta, sel_g, sel_cum, ones = _gdn_selectors()
    kern = functools.partial(_gdn_kernel, tc=tc)
    return pl.pallas_call(
        kern,
        name="gdn",
        grid=(b, t // tc),
        in_specs=[
            pl.BlockSpec((None, tc, 3 * WIDTH), lambda bi, ti: (bi, ti, 0)),
            pl.BlockSpec((DN_CONV, 3 * WIDTH), lambda bi, ti: (0, 0)),
            pl.BlockSpec((None, tc, AB_LANES), lambda bi, ti: (bi, ti, 0)),
            pl.BlockSpec((None, nc, HEADS, 2 * CHUNK), lambda bi, ti: (bi, ti, 0, 0)),
            pl.BlockSpec((1, AB_LANES), lambda bi, ti: (0, 0)),
            pl.BlockSpec((1, AB_LANES), lambda bi, ti: (0, 0)),
            pl.BlockSpec((PAIRS, 2 * CHUNK), lambda bi, ti: (0, 0)),
            pl.BlockSpec((PAIRS, 2 * CHUNK), lambda bi, ti: (0, 0)),
            pl.BlockSpec(sel_beta.shape, lambda bi, ti: (0, 0)),
            pl.BlockSpec(sel_g.shape, lambda bi, ti: (0, 0)),
            pl.BlockSpec(sel_cum.shape, lambda bi, ti: (0, 0)),
            pl.BlockSpec(ones.shape, lambda bi, ti: (0, 0)),
            pl.BlockSpec((None, tc, WIDTH), lambda bi, ti: (bi, ti, COL_DN_GATE * 128 // WIDTH)),
            pl.BlockSpec((1, HEAD_DIM), lambda bi, ti: (0, 0)),
        ],
        out_specs=pl.BlockSpec((None, tc, WIDTH), lambda bi, ti: (bi, ti, 0)),
        out_shape=jax.ShapeDtypeStruct((b, t, WIDTH), BF16),
        scratch_shapes=[
            pltpu.VMEM((HEADS, HEAD_DIM, HEAD_DIM), F32),
            pltpu.VMEM((HALO, 3 * WIDTH), F32),
            pltpu.VMEM((HALO + tc, 3 * WIDTH), F32),
            pltpu.VMEM((tc, 3 * WIDTH), F32),
            pltpu.VMEM((tc, AB_LANES), F32),
            pltpu.VMEM((tc, AB_LANES), F32),
            pltpu.VMEM((nc, PAIRS, 2 * CHUNK), F32),
            pltpu.VMEM((tc, WIDTH), F32),
            pltpu.VMEM((tc, WIDTH), BF16),
            pltpu.VMEM((tc, WIDTH), BF16),
            pltpu.VMEM((tc, WIDTH), BF16),
            pltpu.VMEM((nc * PAIRS, CHUNK, 2 * CHUNK), BF16),
            pltpu.VMEM((nc * HEADS, HALO, HEAD_DIM), F32),
            pltpu.VMEM((nc * PAIRS, CHUNK, 2 * CHUNK), F32),
            pltpu.VMEM((nc * PAIRS, 2 * CHUNK, 2 * HEAD_DIM), BF16),
        ],
        compiler_params=pltpu.CompilerParams(
            dimension_semantics=("parallel", "arbitrary"),
            vmem_limit_bytes=VMEM_LIMIT),
    )(h3, conv_w, ab3, abt, alog_row, dtb_row, alog_pk, dtb_pk, sel_beta, sel_g, sel_cum, ones,
      h3, norm_w.reshape(1, HEAD_DIM))


SB_DEAD_LOG2 = -150.0
SB_Q_SCALE = HEAD_DIM ** -0.5 * math.log2(math.e)


def _sb_kernel(q_ref, k_ref, v_ref, o_ref, *, tq, nh):
    i = pl.program_id(2)
    row = lax.broadcasted_iota(jnp.int32, (tq, tq), 0)
    col = lax.broadcasted_iota(jnp.int32, (tq, tq), 1)
    later = jnp.where(row > col, 1.0, 0.0).astype(BF16)
    hs = range(nh)
    lanes = [slice(h * HEAD_DIM, (h + 1) * HEAD_DIM) for h in hs]
    qs = [q_ref[:, lanes[h]] for h in hs]

    def block(j, accs, runs, mask):
        k0 = pl.multiple_of(j * tq, tq)
        zs = [_dot_nt(qs[h], k_ref[pl.ds(k0, tq), lanes[h]]) for h in hs]
        keeps = []
        for z in zs:
            nz = -z
            keep = jnp.minimum(nz, 0.0) - jnp.log2(1.0 + jnp.exp2(jnp.minimum(z, nz)))
            keeps.append(keep if mask is None else jnp.where(mask, keep, 0.0))
        inner = [_dot(keep.astype(BF16), later) for keep in keeps]
        new_accs, new_runs = [], []
        for h in hs:
            a = jnp.exp2(zs[h] + keeps[h] + inner[h] + runs[h])
            if mask is not None:
                a = jnp.where(mask, a, 0.0)
            new_accs.append(accs[h] + _dot(a.astype(BF16), v_ref[pl.ds(k0, tq), lanes[h]]))
            new_runs.append(runs[h] + jnp.sum(keeps[h], axis=1, keepdims=True))
        return new_accs, new_runs

    accs = [jnp.zeros((tq, HEAD_DIM), F32) for _ in hs]
    runs = [jnp.zeros((tq, 1), F32) for _ in hs]
    accs, runs = block(i, accs, runs, col < row)

    def alive(runs):
        m = runs[0]
        for r in runs[1:]:
            m = jnp.maximum(m, r)
        return jnp.max(m) > SB_DEAD_LOG2

    def cond(carry):
        j, _, _, live = carry
        return jnp.logical_and(j >= 0, live)

    def body(carry):
        j, accs, runs, _ = carry
        accs, runs = block(j, list(accs), list(runs), None)
        return j - 1, tuple(accs), tuple(runs), alive(runs)

    _, accs, _, _ = lax.while_loop(cond, body, (i - 1, tuple(accs), tuple(runs), alive(runs)))
    for h in hs:
        o_ref[:, lanes[h]] = accs[h].astype(o_ref.dtype)


def _sb_attention(h3, tq, nh):
    b, t, _ = h3.shape
    kern = functools.partial(_sb_kernel, tq=tq, nh=nh)
    w = nh * HEAD_DIM
    return pl.pallas_call(
        kern,
        name="sb_attn",
        grid=(b, HEADS // nh, t // tq),
        in_specs=[
            pl.BlockSpec((None, tq, w), lambda bi, hi, qi: (bi, qi, COL_SB_Q // nh + hi)),
            pl.BlockSpec((None, t, w), lambda bi, hi, qi: (bi, 0, COL_SB_K // nh + hi)),
            pl.BlockSpec((None, t, w), lambda bi, hi, qi: (bi, 0, COL_SB_V // nh + hi)),
        ],
        out_specs=pl.BlockSpec((None, tq, w), lambda bi, hi, qi: (bi, qi, hi)),
        out_shape=jax.ShapeDtypeStruct((b, t, WIDTH), BF16),
        compiler_params=pltpu.CompilerParams(
            dimension_semantics=("parallel", "parallel", "arbitrary"),
            vmem_limit_bytes=VMEM_LIMIT),
    )(h3, h3, h3)


def _merge_kernel(x_ref, odn_ref, osb_ref, gdn_ref, gsb_ref, pa_ref, pb_ref, wo_ref, nw_ref,
                  x1_ref, n2_ref):
    pa = _dot(odn_ref[...], pa_ref[...])
    pb = _dot(osb_ref[...], pb_ref[...])
    mixed = (_sigmoid(gdn_ref[...].astype(F32)) * pa + _sigmoid(gsb_ref[...].astype(F32)) * pb)
    x1 = x_ref[...] + _dot(mixed.astype(BF16), wo_ref[...])
    x1_ref[...] = x1
    ms = jnp.mean(x1 * x1, axis=-1, keepdims=True)
    n2_ref[...] = (x1 * lax.rsqrt(ms + EPS) * nw_ref[...]).astype(n2_ref.dtype)


def _merge(x2, odn2, osb2, h2, p_dn, p_sb, w_out, norm_w, tm):
    m = x2.shape[0]
    gcol = COL_MERGE * 128 // D_MODEL
    row_spec = lambda c: pl.BlockSpec((tm, D_MODEL), lambda i: (i, c))
    w_spec = pl.BlockSpec((D_MODEL, D_MODEL), lambda i: (0, 0))
    return pl.pallas_call(
        _merge_kernel,
        name="merge",
        grid=(m // tm,),
        in_specs=[row_spec(0), row_spec(0), row_spec(0), row_spec(gcol), row_spec(gcol + 1),
                  w_spec, w_spec, w_spec, pl.BlockSpec((1, D_MODEL), lambda i: (0, 0))],
        out_specs=[row_spec(0), row_spec(0)],
        out_shape=[jax.ShapeDtypeStruct((m, D_MODEL), F32),
                   jax.ShapeDtypeStruct((m, D_MODEL), BF16)],
        compiler_params=pltpu.CompilerParams(
            dimension_semantics=("parallel",),
            vmem_limit_bytes=VMEM_LIMIT),
    )(x2, odn2, osb2, h2, h2, p_dn, p_sb, w_out, norm_w)


def _ffn_kernel(n2_ref, x1_ref, wup_ref, cw_ref, wd_ref, nw_ref, o_ref,
                ext_ref, carry_ref, *, tm, tf):
    ti = pl.program_id(1)
    n2 = n2_ref[...]

    def conv(cols, slot):
        u = _dot(n2, wup_ref[:, cols])
        ext = ext_ref.at[slot % 2]
        ext[0:HALO, :] = jnp.where(ti == 0, 0.0, carry_ref[slot])
        ext[HALO:HALO + tm, :] = u
        carry_ref[slot] = u[tm - HALO:, :]
        y = cw_ref[2:3, cols] * u
        y = y + cw_ref[1:2, cols] * ext[HALO - 1:HALO - 1 + tm, :]
        y = y + cw_ref[0:1, cols] * ext[HALO - 2:HALO - 2 + tm, :]
        return y

    x2 = x1_ref[...]
    for k in range(D_FF // tf):
        gate = conv(slice(k * tf, (k + 1) * tf), 2 * k)
        up = conv(slice(D_FF + k * tf, D_FF + (k + 1) * tf), 2 * k + 1)
        act = (gate * _sigmoid(gate) * up).astype(BF16)
        x2 = x2 + _dot(act, wd_ref[k * tf:(k + 1) * tf, :])
    ms = jnp.mean(x2 * x2, axis=-1, keepdims=True)
    o_ref[...] = x2 * lax.rsqrt(ms + EPS) * nw_ref[...]


def _ffn(n2, x1, w_up, conv_w, w_down, norm_w, tm, tf):
    b, t, _ = n2.shape
    nk = D_FF // tf
    kern = functools.partial(_ffn_kernel, tm=tm, tf=tf)
    resident = dict(pipeline_mode=pl.Buffered(1))
    return pl.pallas_call(
        kern,
        name="ffn",
        grid=(b, t // tm),
        in_specs=[
            pl.BlockSpec((None, tm, D_MODEL), lambda bi, ti: (bi, ti, 0)),
            pl.BlockSpec((None, tm, D_MODEL), lambda bi, ti: (bi, ti, 0)),
            pl.BlockSpec((D_MODEL, 2 * D_FF), lambda bi, ti: (0, 0), **resident),
            pl.BlockSpec((FFN_CONV, 2 * D_FF), lambda bi, ti: (0, 0), **resident),
            pl.BlockSpec((D_FF, D_MODEL), lambda bi, ti: (0, 0), **resident),
            pl.BlockSpec((1, D_MODEL), lambda bi, ti: (0, 0)),
        ],
        out_specs=pl.BlockSpec((None, tm, D_MODEL), lambda bi, ti: (bi, ti, 0)),
        out_shape=jax.ShapeDtypeStruct((b, t, D_MODEL), F32),
        scratch_shapes=[
            pltpu.VMEM((2, HALO + tm, tf), F32),
            pltpu.VMEM((2 * nk, HALO, tf), F32),
        ],
        compiler_params=pltpu.CompilerParams(
            dimension_semantics=("parallel", "arbitrary"),
            vmem_limit_bytes=VMEM_LIMIT),
    )(n2, x1, w_up, conv_w, w_down, norm_w)


def kernel(x, norm1_w, w_in, dn_conv_w, dn_A_log, dn_dt_bias, dn_norm_w, w_proj_dn, w_proj_sb,
           w_out, norm2_w, ffn_w_up, ffn_conv_w, ffn_w_down, norm_f_w):
    b, t, d = x.shape
    assert w_in.shape[0] == 1, "single-layer block: the final RMSNorm is fused into the FFN call"
    m = b * t
    tm = min(1024, t)
    tc = min(256, t)
    tq = min(256, t)
    qkv_end = 3 * WIDTH
    b_end = qkv_end + 2 * HEADS
    g_end = b_end + WIDTH
    wl = w_in[0]
    w_main = jnp.concatenate(
        [wl[:, :qkv_end], wl[:, b_end:g_end], wl[:, g_end:g_end + WIDTH] * SB_Q_SCALE,
         wl[:, g_end + WIDTH:]], axis=1).astype(BF16)
    w_ab = jnp.pad(wl[:, qkv_end:b_end], ((0, 0), (0, AB_LANES - 2 * HEADS))).astype(BF16)

    h2, ab2 = _in_proj(x.reshape(m, d), norm1_w[0].reshape(1, d), w_main, w_ab, tm, H_COLS // 3)
    h3 = h2.reshape(b, t, H_COLS)
    ab3 = ab2.reshape(b, t, AB_LANES)
    abt = (ab3[:, :, :2 * HEADS].reshape(b, t // CHUNK, CHUNK, 2 * HEADS).transpose(0, 1, 3, 2)
           .reshape(b, t // CHUNK, HEADS, 2 * CHUNK))

    o_dn = _gdn(h3, dn_conv_w[0], ab3, abt, dn_A_log[0], dn_dt_bias[0], dn_norm_w[0], tc)
    o_sb = _sb_attention(h3, tq, 8)

    x1, n2 = _merge(x.reshape(m, d), o_dn.reshape(m, WIDTH), o_sb.reshape(m, WIDTH), h2,
                    w_proj_dn[0].astype(BF16), w_proj_sb[0].astype(BF16),
                    w_out[0].astype(BF16), norm2_w[0].reshape(1, d), min(512, t))

    return _ffn(n2.reshape(b, t, d), x1.reshape(b, t, d), ffn_w_up[0].astype(BF16),
                ffn_conv_w[0], ffn_w_down[0].astype(BF16), norm_f_w.reshape(1, d),
                min(512, t), D_FF // 2)
```

```python
import functools
import math

import numpy as np

import jax
import jax.numpy as jnp
from jax import lax
from jax.experimental import pallas as pl
from jax.experimental.pallas import tpu as pltpu

F32 = jnp.float32
BF16 = jnp.bfloat16

D_MODEL = 1024
HEADS = 8
HEAD_DIM = 128
WIDTH = HEADS * HEAD_DIM
DN_CONV = 4
CHUNK = 64
D_FF = 2816
FFN_CONV = 3
EPS = 1e-6

H_COLS = 3 * WIDTH + WIDTH + 3 * WIDTH + 2 * D_MODEL
COL_DN_GATE = 3 * WIDTH // 128
COL_SB_Q = COL_DN_GATE + WIDTH // 128
COL_SB_K = COL_SB_Q + HEADS
COL_SB_V = COL_SB_K + HEADS
COL_MERGE = COL_SB_V + HEADS
AB_LANES = 128
HALO = 8

VMEM_LIMIT = 56 * 1024 * 1024


def _dot(a, b):
    return jnp.dot(a, b, preferred_element_type=F32)


def _dot_nt(a, b):
    return lax.dot_general(a, b, (((1,), (1,)), ((), ())), preferred_element_type=F32)


def _dot_tn(a, b):
    return lax.dot_general(a, b, (((0,), (0,)), ((), ())), preferred_element_type=F32)


def _sigmoid(x):
    return 1.0 / (1.0 + jnp.exp(-x))


def _softplus(x):
    return jnp.maximum(x, 0.0) + jnp.log(1.0 + jnp.exp(-jnp.abs(x)))


def _in_proj_kernel(x_ref, nw_ref, w_ref, wab_ref, h_ref, ab_ref, *, tn):
    x = x_ref[...]
    ms = jnp.mean(x * x, axis=-1, keepdims=True)
    xn = (x * lax.rsqrt(ms + EPS) * nw_ref[...]).astype(BF16)
    ab_ref[...] = _dot(xn, wab_ref[...])
    for j in range(h_ref.shape[1] // tn):
        cols = slice(j * tn, (j + 1) * tn)
        h_ref[:, cols] = _dot(xn, w_ref[:, cols]).astype(h_ref.dtype)


def _in_proj(x2, norm_w, w_main, w_ab, tm, tn):
    m = x2.shape[0]
    n = w_main.shape[1]
    kern = functools.partial(_in_proj_kernel, tn=tn)
    resident = dict(pipeline_mode=pl.Buffered(1))
    return pl.pallas_call(
        kern,
        name="in_proj",
        grid=(m // tm,),
        in_specs=[
            pl.BlockSpec((tm, D_MODEL), lambda i: (i, 0)),
            pl.BlockSpec((1, D_MODEL), lambda i: (0, 0)),
            pl.BlockSpec((D_MODEL, n), lambda i: (0, 0), **resident),
            pl.BlockSpec((D_MODEL, AB_LANES), lambda i: (0, 0), **resident),
        ],
        out_specs=[
            pl.BlockSpec((tm, n), lambda i: (i, 0)),
            pl.BlockSpec((tm, AB_LANES), lambda i: (i, 0)),
        ],
        out_shape=[
            jax.ShapeDtypeStruct((m, n), BF16),
            jax.ShapeDtypeStruct((m, AB_LANES), F32),
        ],
        compiler_params=pltpu.CompilerParams(
            dimension_semantics=("parallel",),
            vmem_limit_bytes=VMEM_LIMIT),
    )(x2, norm_w, w_main, w_ab)


PAIRS = HEADS // 2
PREP_CHUNKS = 4


def _hi_lo(a):
    hi = a.astype(BF16)
    return hi, (a - hi.astype(F32)).astype(BF16)


def _lhs3(hi, lo):
    return jnp.concatenate([hi, hi, lo], axis=1)


def _rhs3(hi, lo):
    return jnp.concatenate([hi, lo, hi], axis=0)


def _gdn_kernel(qkv_ref, cw_ref, ab_ref, abt_ref, alog_row_ref, dtb_row_ref,
                alog_pk_ref, dtb_pk_ref, sel_beta_ref, sel_g_ref, sel_cum_ref, ones_ref,
                gate_ref, nw_ref, o_ref,
                s_ref, halo_ref, ext_ref, act_ref, gcol_ref, bcol_ref, grow_ref,
                u_ref, w_ref, qg_ref, kd_ref, aqk_ref, dec_ref, l_ref, rhs_ref, *, tc):
    @pl.when(pl.program_id(1) == 0)
    def _():
        s_ref[...] = jnp.zeros_like(s_ref)
        halo_ref[...] = jnp.zeros_like(halo_ref)

    for j in range(3 * WIDTH // HEAD_DIM):
        ls = slice(j * HEAD_DIM, (j + 1) * HEAD_DIM)
        x = qkv_ref[:, ls].astype(F32)
        ext_ref[0:HALO, ls] = halo_ref[:, ls]
        ext_ref[HALO:HALO + tc, ls] = x
        halo_ref[:, ls] = x[tc - HALO:, :]
        y = cw_ref[DN_CONV - 1:DN_CONV, ls] * x
        for tap in range(DN_CONV - 1):
            shift = DN_CONV - 1 - tap
            y = y + cw_ref[tap:tap + 1, ls] * ext_ref[HALO - shift:HALO - shift + tc, ls]
        act_ref[:, ls] = y * _sigmoid(y)

    ab = ab_ref[...]
    gcol_ref[...] = -jnp.exp(alog_row_ref[...]) * _softplus(ab + dtb_row_ref[...])
    bcol_ref[...] = _sigmoid(ab)
    grow_ref[...] = -jnp.exp(alog_pk_ref[...]) * _softplus(abt_ref[:, :PAIRS, :] + dtb_pk_ref[...])

    row = lax.broadcasted_iota(jnp.int32, (CHUNK, 2 * CHUNK), 0)
    lane = lax.broadcasted_iota(jnp.int32, (CHUNK, 2 * CHUNK), 1)
    col = lane & (CHUNK - 1)
    left = lane < CHUNK
    lower = col <= row
    strict = col < row
    eye = jnp.where(row == col, 1.0, 0.0).astype(F32)
    zero_bf = jnp.zeros((CHUNK, 2 * CHUNK), BF16)
    zero_tile = jnp.zeros((CHUNK, HEAD_DIM), BF16)

    def same_block(log2_size):
        return (row >> log2_size) == (col >> log2_size)

    def block_diag(x):
        return jnp.concatenate([jnp.where(left, x, zero_bf), jnp.where(left, zero_bf, x)], axis=0)

    def chunk_inputs(c, carry):
        rows = pl.ds(c * CHUNK, CHUNK)
        beta_all = _dot(jnp.concatenate(_hi_lo(bcol_ref[rows, :]), axis=1), sel_beta_ref[...])
        g_pairs = _dot(jnp.concatenate(_hi_lo(gcol_ref[rows, :]), axis=1), sel_g_ref[...])
        g_rows = grow_ref[c]
        probs = [dict(c=c, rows=rows, pi=pi, beta_all=beta_all,
                      g_c=g_pairs[:, pi * 2 * CHUNK:(pi + 1) * 2 * CHUNK],
                      g_r=g_rows[pi:pi + 1, :]) for pi in range(PAIRS)]
        for p in probs:
            rows, pi = p["rows"], p["pi"]
            hs = (2 * pi, 2 * pi + 1)
            masked = jnp.where(lower, p["g_r"], 0.0)
            cum = _dot(jnp.concatenate(_hi_lo(masked), axis=1), sel_cum_ref[...])
            gc_c = cum[:, :2 * CHUNK]
            gfull = [cum[:, HEAD_DIM:2 * HEAD_DIM], cum[:, 2 * HEAD_DIM:]]
            gc_r = jnp.sum(jnp.where(row <= col, p["g_c"], 0.0), axis=0, keepdims=True)
            decay = jnp.where(lower, jnp.exp(jnp.where(lower, gc_c - gc_r, 0.0)), 0.0)
            qs, ks, kbs, rhs = [], [], [], []
            for side, h in enumerate(hs):
                lanes = slice(h * HEAD_DIM, (h + 1) * HEAD_DIM)
                q = act_ref[rows, h * HEAD_DIM:(h + 1) * HEAD_DIM]
                k = act_ref[rows, WIDTH + h * HEAD_DIM:WIDTH + (h + 1) * HEAD_DIM]
                v = act_ref[rows, 2 * WIDTH + h * HEAD_DIM:2 * WIDTH + (h + 1) * HEAD_DIM]
                sumsq = _dot(jnp.concatenate([q * q, k * k], axis=0).astype(BF16), ones_ref[...])
                q = q * (lax.rsqrt(sumsq[:CHUNK] + EPS) * (HEAD_DIM ** -0.5))
                k = k * lax.rsqrt(sumsq[CHUNK:] + EPS)
                beta = p["beta_all"][:, lanes]
                eg = jnp.exp(gfull[side])
                g_last = gfull[side][CHUNK - 1:CHUNK, :]
                kb = k * beta
                rhs.append(jnp.concatenate([v * beta, kb * eg], axis=1).astype(BF16))
                qs.append(q.astype(BF16))
                ks.append(k.astype(BF16))
                kbs.append(kb.astype(BF16))
                dec_ref[p["c"] * HEADS + h] = jnp.broadcast_to(jnp.exp(g_last), (HALO, HEAD_DIM))
                qg_ref[rows, lanes] = (q * eg).astype(BF16)
                kd_ref[rows, lanes] = (k * jnp.exp(g_last - gfull[side])).astype(BF16)
            rhs_ref[p["c"] * PAIRS + pi] = jnp.concatenate(rhs, axis=0)
            kq = jnp.concatenate([jnp.concatenate(kbs, axis=1), jnp.concatenate(qs, axis=1)], axis=0)
            kt = jnp.concatenate([jnp.concatenate([ks[0], zero_tile], axis=1),
                                  jnp.concatenate([zero_tile, ks[1]], axis=1)], axis=0)
            kk_qk = _dot_nt(kq, kt)
            l_ref[p["c"] * PAIRS + pi] = jnp.where(strict, kk_qk[:CHUNK] * decay, 0.0)
            aqk_ref[p["c"] * PAIRS + pi] = (kk_qk[CHUNK:] * decay).astype(BF16)
        return carry

    def solve(chunks):
        probs = [dict(c=c, pi=pi, rows=pl.ds(c * CHUNK, CHUNK), l=l_ref[c * PAIRS + pi])
                 for c in chunks for pi in range(PAIRS)]
        invs = [eye - jnp.where(same_block(1), p["l"], 0.0) for p in probs]
        for lvl in range(1, 6):
            off_diag = same_block(lvl + 1) & jnp.logical_not(same_block(lvl))
            cs = [jnp.where(off_diag, p["l"], 0.0).astype(BF16) for p in probs]
            ds = [d.astype(BF16) for d in invs]
            xs = [_dot(d, block_diag(c)).astype(BF16) for d, c in zip(ds, cs)]
            invs = [inv - _dot(x, block_diag(d)) for inv, x, d in zip(invs, xs, ds)]
        t0s = [_hi_lo(t) for t in invs]
        ls = [_hi_lo(p["l"]) for p in probs]
        resid = [(eye - t) - _dot(_lhs3(*l), _rhs3(block_diag(t0[0]), block_diag(t0[1])))
                 for t, t0, l in zip(invs, t0s, ls)]
        invs = [t + _dot(t0[0], block_diag(r.astype(BF16))) for t, t0, r in zip(invs, t0s, resid)]
        for p, inv in zip(probs, invs):
            hi, lo = _hi_lo(inv)
            rhs = rhs_ref[p["c"] * PAIRS + p["pi"]]
            rhs = jnp.concatenate([rhs, rhs], axis=0)
            for side in range(2):
                h = 2 * p["pi"] + side
                lanes = slice(h * HEAD_DIM, (h + 1) * HEAD_DIM)
                keep = left if side == 0 else jnp.logical_not(left)
                t_hi_lo = jnp.concatenate([jnp.where(keep, hi, zero_bf), jnp.where(keep, lo, zero_bf)], axis=1)
                uw = _dot(t_hi_lo, rhs)
                u_ref[p["rows"], lanes] = uw[:, :HEAD_DIM]
                w_ref[p["rows"], lanes] = uw[:, HEAD_DIM:].astype(BF16)

    for c in range(tc // CHUNK):
        chunk_inputs(c, 0)
    for i in range(tc // (PREP_CHUNKS * CHUNK)):
        solve([PREP_CHUNKS * i + j for j in range(PREP_CHUNKS)])

    nw = nw_ref[...]

    def recur_body(c, carry):
        rows = pl.ds(c * CHUNK, CHUNK)
        hs = range(HEADS)
        lanes = [slice(h * HEAD_DIM, (h + 1) * HEAD_DIM) for h in hs]
        s_old = [s_ref[h] for h in hs]
        ws_qs = [_dot(jnp.concatenate([w_ref[rows, lanes[h]], qg_ref[rows, lanes[h]]], axis=0),
                      s_old[h].astype(BF16)) for h in hs]
        v_new = [(u_ref[rows, lanes[h]] - ws_qs[h][:CHUNK]).astype(BF16) for h in hs]
        for h in hs:
            s_ref[h] = (s_old[h] * dec_ref[c * HEADS + h][0:1, :]
                        + _dot_tn(kd_ref[rows, lanes[h]], v_new[h]))
        intra = []
        for pi in range(PAIRS):
            v_bd = jnp.concatenate([jnp.concatenate([v_new[2 * pi], zero_tile], axis=1),
                                    jnp.concatenate([zero_tile, v_new[2 * pi + 1]], axis=1)], axis=0)
            av = _dot(aqk_ref[c * PAIRS + pi], v_bd)
            intra += [av[:, :HEAD_DIM], av[:, HEAD_DIM:]]
        for h in hs:
            o = ws_qs[h][CHUNK:] + intra[h]
            o = o * lax.rsqrt(jnp.mean(o * o, axis=-1, keepdims=True) + EPS) * nw
            gate = gate_ref[rows, lanes[h]].astype(F32)
            o_ref[rows, lanes[h]] = (o * (gate * _sigmoid(gate))).astype(o_ref.dtype)
        return carry

    for c in range(tc // CHUNK):
        recur_body(c, 0)


def _gdn_selectors():
    src = np.arange(2 * AB_LANES)[:, None] % AB_LANES
    out = np.arange(HEADS * HEAD_DIM)[None, :]
    sel_beta = src == HEADS + out // HEAD_DIM
    out_g = out[:, :PAIRS * 2 * CHUNK]
    sel_g = src == 2 * (out_g // (2 * CHUNK)) + (out_g % (2 * CHUNK)) // CHUNK
    out_c = np.arange(3 * HEAD_DIM)[None, :]
    src_left = src < CHUNK
    sel_cum = np.where(out_c < HEAD_DIM, src_left == (out_c < CHUNK),
                       np.where(out_c < 2 * HEAD_DIM, src_left, ~src_left))
    ones = np.ones((HEAD_DIM, HEAD_DIM))
    return tuple(jnp.asarray(a, BF16) for a in (sel_beta, sel_g, sel_cum, ones))


def _gdn(h3, conv_w, ab3, abt, alog, dtb, norm_w, tc):
    b, t, _ = h3.shape
    nc = tc // CHUNK
    assert nc % PREP_CHUNKS == 0
    alog_row = jnp.zeros((1, AB_LANES), F32).at[0, :HEADS].set(alog)
    dtb_row = jnp.zeros((1, AB_LANES), F32).at[0, :HEADS].set(dtb)
    alog_pk = jnp.repeat(alog, CHUNK).reshape(PAIRS, 2 * CHUNK)
    dtb_pk = jnp.repeat(dtb, CHUNK).reshape(PAIRS, 2 * CHUNK)
    sel_beta, sel_g, sel_cum, ones = _gdn_selectors()
    kern = functools.partial(_gdn_kernel, tc=tc)
    return pl.pallas_call(
        kern,
        name="gdn",
        grid=(b, t // tc),
        in_specs=[
            pl.BlockSpec((None, tc, 3 * WIDTH), lambda bi, ti: (bi, ti, 0)),
            pl.BlockSpec((DN_CONV, 3 * WIDTH), lambda bi, ti: (0, 0)),
            pl.BlockSpec((None, tc, AB_LANES), lambda bi, ti: (bi, ti, 0)),
            pl.BlockSpec((None, nc, HEADS, 2 * CHUNK), lambda bi, ti: (bi, ti, 0, 0)),
            pl.BlockSpec((1, AB_LANES), lambda bi, ti: (0, 0)),
            pl.BlockSpec((1, AB_LANES), lambda bi, ti: (0, 0)),
            pl.BlockSpec((PAIRS, 2 * CHUNK), lambda bi, ti: (0, 0)),
            pl.BlockSpec((PAIRS, 2 * CHUNK), lambda bi, ti: (0, 0)),
            pl.BlockSpec(sel_beta.shape, lambda bi, ti: (0, 0)),
            pl.BlockSpec(sel_g.shape, lambda bi, ti: (0, 0)),
            pl.BlockSpec(sel_cum.shape, lambda bi, ti: (0, 0)),
            pl.BlockSpec(ones.shape, lambda bi, ti: (0, 0)),
            pl.BlockSpec((None, tc, WIDTH), lambda bi, ti: (bi, ti, COL_DN_GATE * 128 // WIDTH)),
            pl.BlockSpec((1, HEAD_DIM), lambda bi, ti: (0, 0)),
        ],
        out_specs=pl.BlockSpec((None, tc, WIDTH), lambda bi, ti: (bi, ti, 0)),
        out_shape=jax.ShapeDtypeStruct((b, t, WIDTH), BF16),
        scratch_shapes=[
            pltpu.VMEM((HEADS, HEAD_DIM, HEAD_DIM), F32),
            pltpu.VMEM((HALO, 3 * WIDTH), F32),
            pltpu.VMEM((HALO + tc, 3 * WIDTH), F32),
            pltpu.VMEM((tc, 3 * WIDTH), F32),
            pltpu.VMEM((tc, AB_LANES), F32),
            pltpu.VMEM((tc, AB_LANES), F32),
            pltpu.VMEM((nc, PAIRS, 2 * CHUNK), F32),
            pltpu.VMEM((tc, WIDTH), F32),
            pltpu.VMEM((tc, WIDTH), BF16),
            pltpu.VMEM((tc, WIDTH), BF16),
            pltpu.VMEM((tc, WIDTH), BF16),
            pltpu.VMEM((nc * PAIRS, CHUNK, 2 * CHUNK), BF16),
            pltpu.VMEM((nc * HEADS, HALO, HEAD_DIM), F32),
            pltpu.VMEM((nc * PAIRS, CHUNK, 2 * CHUNK), F32),
            pltpu.VMEM((nc * PAIRS, 2 * CHUNK, 2 * HEAD_DIM), BF16),
        ],
        compiler_params=pltpu.CompilerParams(
            dimension_semantics=("parallel", "arbitrary"),
            vmem_limit_bytes=VMEM_LIMIT),
    )(h3, conv_w, ab3, abt, alog_row, dtb_row, alog_pk, dtb_pk, sel_beta, sel_g, sel_cum, ones,
      h3, norm_w.reshape(1, HEAD_DIM))


SB_DEAD_LOG2 = -150.0
SB_Q_SCALE = HEAD_DIM ** -0.5 * math.log2(math.e)


def _sb_kernel(q_ref, k_ref, v_ref, o_ref, *, tq, nh):
    i = pl.program_id(2)
    row = lax.broadcasted_iota(jnp.int32, (tq, tq), 0)
    col = lax.broadcasted_iota(jnp.int32, (tq, tq), 1)
    later = jnp.where(row > col, 1.0, 0.0).astype(BF16)
    hs = range(nh)
    lanes = [slice(h * HEAD_DIM, (h + 1) * HEAD_DIM) for h in hs]
    qs = [q_ref[:, lanes[h]] for h in hs]

    def block(j, accs, runs, mask):
        k0 = pl.multiple_of(j * tq, tq)
        zs = [_dot_nt(qs[h], k_ref[pl.ds(k0, tq), lanes[h]]) for h in hs]
        keeps = []
        for z in zs:
            nz = -z
            keep = jnp.minimum(nz, 0.0) - jnp.log2(1.0 + jnp.exp2(jnp.minimum(z, nz)))
            keeps.append(keep if mask is None else jnp.where(mask, keep, 0.0))
        inner = [_dot(keep.astype(BF16), later) for keep in keeps]
        new_accs, new_runs = [], []
        for h in hs:
            a = jnp.exp2(zs[h] + keeps[h] + inner[h] + runs[h])
            if mask is not None:
                a = jnp.where(mask, a, 0.0)
            new_accs.append(accs[h] + _dot(a.astype(BF16), v_ref[pl.ds(k0, tq), lanes[h]]))
            new_runs.append(runs[h] + jnp.sum(keeps[h], axis=1, keepdims=True))
        return new_accs, new_runs

    accs = [jnp.zeros((tq, HEAD_DIM), F32) for _ in hs]
    runs = [jnp.zeros((tq, 1), F32) for _ in hs]
    accs, runs = block(i, accs, runs, col < row)

    def alive(runs):
        m = runs[0]
        for r in runs[1:]:
            m = jnp.maximum(m, r)
        return jnp.max(m) > SB_DEAD_LOG2

    def cond(carry):
        j, _, _, live = carry
        return jnp.logical_and(j >= 0, live)

    def body(carry):
        j, accs, runs, _ = carry
        accs, runs = block(j, list(accs), list(runs), None)
        return j - 1, tuple(accs), tuple(runs), alive(runs)

    _, accs, _, _ = lax.while_loop(cond, body, (i - 1, tuple(accs), tuple(runs), alive(runs)))
    for h in hs:
        o_ref[:, lanes[h]] = accs[h].astype(o_ref.dtype)


def _sb_attention(h3, tq, nh):
    b, t, _ = h3.shape
    kern = functools.partial(_sb_kernel, tq=tq, nh=nh)
    w = nh * HEAD_DIM
    return pl.pallas_call(
        kern,
        name="sb_attn",
        grid=(b, HEADS // nh, t // tq),
        in_specs=[
            pl.BlockSpec((None, tq, w), lambda bi, hi, qi: (bi, qi, COL_SB_Q // nh + hi)),
            pl.BlockSpec((None, t, w), lambda bi, hi, qi: (bi, 0, COL_SB_K // nh + hi)),
            pl.BlockSpec((None, t, w), lambda bi, hi, qi: (bi, 0, COL_SB_V // nh + hi)),
        ],
        out_specs=pl.BlockSpec((None, tq, w), lambda bi, hi, qi: (bi, qi, hi)),
        out_shape=jax.ShapeDtypeStruct((b, t, WIDTH), BF16),
        compiler_params=pltpu.CompilerParams(
            dimension_semantics=("parallel", "parallel", "arbitrary"),
            vmem_limit_bytes=VMEM_LIMIT),
    )(h3, h3, h3)


def _merge_kernel(x_ref, odn_ref, osb_ref, gdn_ref, gsb_ref, pa_ref, pb_ref, wo_ref, nw_ref,
                  x1_ref, n2_ref):
    pa = _dot(odn_ref[...], pa_ref[...])
    pb = _dot(osb_ref[...], pb_ref[...])
    mixed = (_sigmoid(gdn_ref[...].astype(F32)) * pa + _sigmoid(gsb_ref[...].astype(F32)) * pb)
    x1 = x_ref[...] + _dot(mixed.astype(BF16), wo_ref[...])
    x1_ref[...] = x1
    ms = jnp.mean(x1 * x1, axis=-1, keepdims=True)
    n2_ref[...] = (x1 * lax.rsqrt(ms + EPS) * nw_ref[...]).astype(n2_ref.dtype)


def _merge(x2, odn2, osb2, h2, p_dn, p_sb, w_out, norm_w, tm):
    m = x2.shape[0]
    gcol = COL_MERGE * 128 // D_MODEL
    row_spec = lambda c: pl.BlockSpec((tm, D_MODEL), lambda i: (i, c))
    w_spec = pl.BlockSpec((D_MODEL, D_MODEL), lambda i: (0, 0))
    return pl.pallas_call(
        _merge_kernel,
        name="merge",
        grid=(m // tm,),
        in_specs=[row_spec(0), row_spec(0), row_spec(0), row_spec(gcol), row_spec(gcol + 1),
                  w_spec, w_spec, w_spec, pl.BlockSpec((1, D_MODEL), lambda i: (0, 0))],
        out_specs=[row_spec(0), row_spec(0)],
        out_shape=[jax.ShapeDtypeStruct((m, D_MODEL), F32),
                   jax.ShapeDtypeStruct((m, D_MODEL), BF16)],
        compiler_params=pltpu.CompilerParams(
            dimension_semantics=("parallel",),
            vmem_limit_bytes=VMEM_LIMIT),
    )(x2, odn2, osb2, h2, h2, p_dn, p_sb, w_out, norm_w)


def _ffn_kernel(n2_ref, x1_ref, wup_ref, cw_ref, wd_ref, nw_ref, o_ref,
                ext_ref, carry_ref, *, tm, tf):
    ti = pl.program_id(1)
    n2 = n2_ref[...]

    def conv(cols, slot):
        u = _dot(n2, wup_ref[:, cols])
        ext = ext_ref.at[slot % 2]
        ext[0:HALO, :] = jnp.where(ti == 0, 0.0, carry_ref[slot])
        ext[HALO:HALO + tm, :] = u
        carry_ref[slot] = u[tm - HALO:, :]
        y = cw_ref[2:3, cols] * u
        y = y + cw_ref[1:2, cols] * ext[HALO - 1:HALO - 1 + tm, :]
        y = y + cw_ref[0:1, cols] * ext[HALO - 2:HALO - 2 + tm, :]
        return y

    x2 = x1_ref[...]
    for k in range(D_FF // tf):
        gate = conv(slice(k * tf, (k + 1) * tf), 2 * k)
        up = conv(slice(D_FF + k * tf, D_FF + (k + 1) * tf), 2 * k + 1)
        act = (gate * _sigmoid(gate) * up).astype(BF16)
        x2 = x2 + _dot(act, wd_ref[k * tf:(k + 1) * tf, :])
    ms = jnp.mean(x2 * x2, axis=-1, keepdims=True)
    o_ref[...] = x2 * lax.rsqrt(ms + EPS) * nw_ref[...]


def _ffn(n2, x1, w_up, conv_w, w_down, norm_w, tm, tf):
    b, t, _ = n2.shape
    nk = D_FF // tf
    kern = functools.partial(_ffn_kernel, tm=tm, tf=tf)
    resident = dict(pipeline_mode=pl.Buffered(1))
    return pl.pallas_call(
        kern,
        name="ffn",
        grid=(b, t // tm),
        in_specs=[
            pl.BlockSpec((None, tm, D_MODEL), lambda bi, ti: (bi, ti, 0)),
            pl.BlockSpec((None, tm, D_MODEL), lambda bi, ti: (bi, ti, 0)),
            pl.BlockSpec((D_MODEL, 2 * D_FF), lambda bi, ti: (0, 0), **resident),
            pl.BlockSpec((FFN_CONV, 2 * D_FF), lambda bi, ti: (0, 0), **resident),
            pl.BlockSpec((D_FF, D_MODEL), lambda bi, ti: (0, 0), **resident),
            pl.BlockSpec((1, D_MODEL), lambda bi, ti: (0, 0)),
        ],
        out_specs=pl.BlockSpec((None, tm, D_MODEL), lambda bi, ti: (bi, ti, 0)),
        out_shape=jax.ShapeDtypeStruct((b, t, D_MODEL), F32),
        scratch_shapes=[
            pltpu.VMEM((2, HALO + tm, tf), F32),
            pltpu.VMEM((2 * nk, HALO, tf), F32),
        ],
        compiler_params=pltpu.CompilerParams(
            dimension_semantics=("parallel", "arbitrary"),
            vmem_limit_bytes=VMEM_LIMIT),
    )(n2, x1, w_up, conv_w, w_down, norm_w)


def kernel(x, norm1_w, w_in, dn_conv_w, dn_A_log, dn_dt_bias, dn_norm_w, w_proj_dn, w_proj_sb,
           w_out, norm2_w, ffn_w_up, ffn_conv_w, ffn_w_down, norm_f_w):
    b, t, d = x.shape
    assert w_in.shape[0] == 1, "single-layer block: the final RMSNorm is fused into the FFN call"
    m = b * t
    tm = min(512, t)
    tc = min(256, t)
    tq = min(256, t)
    qkv_end = 3 * WIDTH
    b_end = qkv_end + 2 * HEADS
    g_end = b_end + WIDTH
    wl = w_in[0]
    w_main = jnp.concatenate(
        [wl[:, :qkv_end], wl[:, b_end:g_end], wl[:, g_end:g_end + WIDTH] * SB_Q_SCALE,
         wl[:, g_end + WIDTH:]], axis=1).astype(BF16)
    w_ab = jnp.pad(wl[:, qkv_end:b_end], ((0, 0), (0, AB_LANES - 2 * HEADS))).astype(BF16)

    h2, ab2 = _in_proj(x.reshape(m, d), norm1_w[0].reshape(1, d), w_main, w_ab, tm, H_COLS // 3)
    h3 = h2.reshape(b, t, H_COLS)
    ab3 = ab2.reshape(b, t, AB_LANES)
    abt = (ab3[:, :, :2 * HEADS].reshape(b, t // CHUNK, CHUNK, 2 * HEADS).transpose(0, 1, 3, 2)
           .reshape(b, t // CHUNK, HEADS, 2 * CHUNK))

    o_dn = _gdn(h3, dn_conv_w[0], ab3, abt, dn_A_log[0], dn_dt_bias[0], dn_norm_w[0], tc)
    o_sb = _sb_attention(h3, tq, 8)

    x1, n2 = _merge(x.reshape(m, d), o_dn.reshape(m, WIDTH), o_sb.reshape(m, WIDTH), h2,
                    w_proj_dn[0].astype(BF16), w_proj_sb[0].astype(BF16),
                    w_out[0].astype(BF16), norm2_w[0].reshape(1, d), min(512, t))

    return _ffn(n2.reshape(b, t, d), x1.reshape(b, t, d), ffn_w_up[0].astype(BF16),
                ffn_conv_w[0], ffn_w_down[0].astype(BF16), norm_f_w.reshape(1, d),
                min(512, t), D_FF // 2)
```

```python
import functools
import math

import numpy as np

import jax
import jax.numpy as jnp
from jax import lax
from jax.experimental import pallas as pl
from jax.experimental.pallas import tpu as pltpu

F32 = jnp.float32
BF16 = jnp.bfloat16

D_MODEL = 1024
HEADS = 8
HEAD_DIM = 128
WIDTH = HEADS * HEAD_DIM
DN_CONV = 4
CHUNK = 64
D_FF = 2816
FFN_CONV = 3
EPS = 1e-6

H_COLS = 3 * WIDTH + WIDTH + 3 * WIDTH + 2 * D_MODEL
COL_DN_GATE = 3 * WIDTH // 128
COL_SB_Q = COL_DN_GATE + WIDTH // 128
COL_SB_K = COL_SB_Q + HEADS
COL_SB_V = COL_SB_K + HEADS
COL_MERGE = COL_SB_V + HEADS
AB_LANES = 128
HALO = 8

VMEM_LIMIT = 56 * 1024 * 1024


def _dot(a, b):
    return jnp.dot(a, b, preferred_element_type=F32)


def _dot_nt(a, b):
    return lax.dot_general(a, b, (((1,), (1,)), ((), ())), preferred_element_type=F32)


def _dot_tn(a, b):
    return lax.dot_general(a, b, (((0,), (0,)), ((), ())), preferred_element_type=F32)


def _sigmoid(x):
    return 1.0 / (1.0 + jnp.exp(-x))


def _softplus(x):
    return jnp.maximum(x, 0.0) + jnp.log(1.0 + jnp.exp(-jnp.abs(x)))


def _in_proj_kernel(x_ref, nw_ref, w_ref, wab_ref, h_ref, ab_ref, *, tn):
    x = x_ref[...]
    ms = jnp.mean(x * x, axis=-1, keepdims=True)
    xn = (x * lax.rsqrt(ms + EPS) * nw_ref[...]).astype(BF16)
    ab_ref[...] = _dot(xn, wab_ref[...])
    for j in range(h_ref.shape[1] // tn):
        cols = slice(j * tn, (j + 1) * tn)
        h_ref[:, cols] = _dot(xn, w_ref[:, cols]).astype(h_ref.dtype)


def _in_proj(x2, norm_w, w_main, w_ab, tm, tn):
    m = x2.shape[0]
    n = w_main.shape[1]
    kern = functools.partial(_in_proj_kernel, tn=tn)
    resident = dict(pipeline_mode=pl.Buffered(1))
    return pl.pallas_call(
        kern,
        name="in_proj",
        grid=(m // tm,),
        in_specs=[
            pl.BlockSpec((tm, D_MODEL), lambda i: (i, 0)),
            pl.BlockSpec((1, D_MODEL), lambda i: (0, 0)),
            pl.BlockSpec((D_MODEL, n), lambda i: (0, 0), **resident),
            pl.BlockSpec((D_MODEL, AB_LANES), lambda i: (0, 0), **resident),
        ],
        out_specs=[
            pl.BlockSpec((tm, n), lambda i: (i, 0)),
            pl.BlockSpec((tm, AB_LANES), lambda i: (i, 0)),
        ],
        out_shape=[
            jax.ShapeDtypeStruct((m, n), BF16),
            jax.ShapeDtypeStruct((m, AB_LANES), F32),
        ],
        compiler_params=pltpu.CompilerParams(
            dimension_semantics=("parallel",),
            vmem_limit_bytes=VMEM_LIMIT),
    )(x2, norm_w, w_main, w_ab)


PAIRS = HEADS // 2
PREP_CHUNKS = 4


def _hi_lo(a):
    hi = a.astype(BF16)
    return hi, (a - hi.astype(F32)).astype(BF16)


def _lhs3(hi, lo):
    return jnp.concatenate([hi, hi, lo], axis=1)


def _rhs3(hi, lo):
    return jnp.concatenate([hi, lo, hi], axis=0)


def _gdn_kernel(qkv_ref, cw_ref, ab_ref, abt_ref, alog_row_ref, dtb_row_ref,
                alog_pk_ref, dtb_pk_ref, sel_beta_ref, sel_g_ref, sel_cum_ref, ones_ref,
                gate_ref, nw_ref, o_ref,
                s_ref, halo_ref, ext_ref, act_ref, gcol_ref, bcol_ref, grow_ref,
                u_ref, w_ref, qg_ref, kd_ref, aqk_ref, dec_ref, l_ref, rhs_ref, *, tc):
    @pl.when(pl.program_id(1) == 0)
    def _():
        s_ref[...] = jnp.zeros_like(s_ref)
        halo_ref[...] = jnp.zeros_like(halo_ref)

    for j in range(3 * WIDTH // HEAD_DIM):
        ls = slice(j * HEAD_DIM, (j + 1) * HEAD_DIM)
        x = qkv_ref[:, ls].astype(F32)
        ext_ref[0:HALO, ls] = halo_ref[:, ls]
        ext_ref[HALO:HALO + tc, ls] = x
        halo_ref[:, ls] = x[tc - HALO:, :]
        y = cw_ref[DN_CONV - 1:DN_CONV, ls] * x
        for tap in range(DN_CONV - 1):
            shift = DN_CONV - 1 - tap
            y = y + cw_ref[tap:tap + 1, ls] * ext_ref[HALO - shift:HALO - shift + tc, ls]
        act_ref[:, ls] = y * _sigmoid(y)

    ab = ab_ref[...]
    gcol_ref[...] = -jnp.exp(alog_row_ref[...]) * _softplus(ab + dtb_row_ref[...])
    bcol_ref[...] = _sigmoid(ab)
    grow_ref[...] = -jnp.exp(alog_pk_ref[...]) * _softplus(abt_ref[:, :PAIRS, :] + dtb_pk_ref[...])

    row = lax.broadcasted_iota(jnp.int32, (CHUNK, 2 * CHUNK), 0)
    lane = lax.broadcasted_iota(jnp.int32, (CHUNK, 2 * CHUNK), 1)
    col = lane & (CHUNK - 1)
    left = lane < CHUNK
    lower = col <= row
    strict = col < row
    eye = jnp.where(row == col, 1.0, 0.0).astype(F32)
    zero_bf = jnp.zeros((CHUNK, 2 * CHUNK), BF16)
    zero_tile = jnp.zeros((CHUNK, HEAD_DIM), BF16)

    def same_block(log2_size):
        return (row >> log2_size) == (col >> log2_size)

    def block_diag(x):
        return jnp.concatenate([jnp.where(left, x, zero_bf), jnp.where(left, zero_bf, x)], axis=0)

    def chunk_inputs(c, carry):
        rows = pl.ds(c * CHUNK, CHUNK)
        beta_all = _dot(jnp.concatenate(_hi_lo(bcol_ref[rows, :]), axis=1), sel_beta_ref[...])
        g_pairs = _dot(jnp.concatenate(_hi_lo(gcol_ref[rows, :]), axis=1), sel_g_ref[...])
        g_rows = grow_ref[c]
        probs = [dict(c=c, rows=rows, pi=pi, beta_all=beta_all,
                      g_c=g_pairs[:, pi * 2 * CHUNK:(pi + 1) * 2 * CHUNK],
                      g_r=g_rows[pi:pi + 1, :]) for pi in range(PAIRS)]
        for p in probs:
            rows, pi = p["rows"], p["pi"]
            hs = (2 * pi, 2 * pi + 1)
            masked = jnp.where(lower, p["g_r"], 0.0)
            cum = _dot(jnp.concatenate(_hi_lo(masked), axis=1), sel_cum_ref[...])
            gc_c = cum[:, :2 * CHUNK]
            gfull = [cum[:, HEAD_DIM:2 * HEAD_DIM], cum[:, 2 * HEAD_DIM:]]
            gc_r = jnp.sum(jnp.where(row <= col, p["g_c"], 0.0), axis=0, keepdims=True)
            decay = jnp.where(lower, jnp.exp(jnp.where(lower, gc_c - gc_r, 0.0)), 0.0)
            qs, ks, kbs, rhs = [], [], [], []
            for side, h in enumerate(hs):
                lanes = slice(h * HEAD_DIM, (h + 1) * HEAD_DIM)
                q = act_ref[rows, h * HEAD_DIM:(h + 1) * HEAD_DIM]
                k = act_ref[rows, WIDTH + h * HEAD_DIM:WIDTH + (h + 1) * HEAD_DIM]
                v = act_ref[rows, 2 * WIDTH + h * HEAD_DIM:2 * WIDTH + (h + 1) * HEAD_DIM]
                sumsq = _dot(jnp.concatenate([q * q, k * k], axis=0).astype(BF16), ones_ref[...])
                q = q * (lax.rsqrt(sumsq[:CHUNK] + EPS) * (HEAD_DIM ** -0.5))
                k = k * lax.rsqrt(sumsq[CHUNK:] + EPS)
                beta = p["beta_all"][:, lanes]
                eg = jnp.exp(gfull[side])
                g_last = gfull[side][CHUNK - 1:CHUNK, :]
                kb = k * beta
                rhs.append(jnp.concatenate([v * beta, kb * eg], axis=1).astype(BF16))
                qs.append(q.astype(BF16))
                ks.append(k.astype(BF16))
                kbs.append(kb.astype(BF16))
                dec_ref[p["c"] * HEADS + h] = jnp.broadcast_to(jnp.exp(g_last), (HALO, HEAD_DIM))
                qg_ref[rows, lanes] = (q * eg).astype(BF16)
                kd_ref[rows, lanes] = (k * jnp.exp(g_last - gfull[side])).astype(BF16)
            rhs_ref[p["c"] * PAIRS + pi] = jnp.concatenate(rhs, axis=0)
            kq = jnp.concatenate([jnp.concatenate(kbs, axis=1), jnp.concatenate(qs, axis=1)], axis=0)
            kt = jnp.concatenate([jnp.concatenate([ks[0], zero_tile], axis=1),
                                  jnp.concatenate([zero_tile, ks[1]], axis=1)], axis=0)
            kk_qk = _dot_nt(kq, kt)
            l_ref[p["c"] * PAIRS + pi] = jnp.where(strict, kk_qk[:CHUNK] * decay, 0.0)
            aqk_ref[p["c"] * PAIRS + pi] = (kk_qk[CHUNK:] * decay).astype(BF16)
        return carry

    def solve(chunks):
        probs = [dict(c=c, pi=pi, rows=pl.ds(c * CHUNK, CHUNK), l=l_ref[c * PAIRS + pi])
                 for c in chunks for pi in range(PAIRS)]
        invs = [eye - jnp.where(same_block(1), p["l"], 0.0) for p in probs]
        for lvl in range(1, 6):
            off_diag = same_block(lvl + 1) & jnp.logical_not(same_block(lvl))
            cs = [jnp.where(off_diag, p["l"], 0.0).astype(BF16) for p in probs]
            ds = [d.astype(BF16) for d in invs]
            xs = [_dot(d, block_diag(c)).astype(BF16) for d, c in zip(ds, cs)]
            invs = [inv - _dot(x, block_diag(d)) for inv, x, d in zip(invs, xs, ds)]
        t0s = [_hi_lo(t) for t in invs]
        ls = [_hi_lo(p["l"]) for p in probs]
        resid = [(eye - t) - _dot(_lhs3(*l), _rhs3(block_diag(t0[0]), block_diag(t0[1])))
                 for t, t0, l in zip(invs, t0s, ls)]
        invs = [t + _dot(t0[0], block_diag(r.astype(BF16))) for t, t0, r in zip(invs, t0s, resid)]
        for p, inv in zip(probs, invs):
            hi, lo = _hi_lo(inv)
            rhs = rhs_ref[p["c"] * PAIRS + p["pi"]]
            rhs = jnp.concatenate([rhs, rhs], axis=0)
            for side in range(2):
                h = 2 * p["pi"] + side
                lanes = slice(h * HEAD_DIM, (h + 1) * HEAD_DIM)
                keep = left if side == 0 else jnp.logical_not(left)
                t_hi_lo = jnp.concatenate([jnp.where(keep, hi, zero_bf), jnp.where(keep, lo, zero_bf)], axis=1)
                uw = _dot(t_hi_lo, rhs)
                u_ref[p["rows"], lanes] = uw[:, :HEAD_DIM]
                w_ref[p["rows"], lanes] = uw[:, HEAD_DIM:].astype(BF16)

    for c in range(tc // CHUNK):
        chunk_inputs(c, 0)
    for i in range(tc // (PREP_CHUNKS * CHUNK)):
        solve([PREP_CHUNKS * i + j for j in range(PREP_CHUNKS)])

    nw = nw_ref[...]

    def recur_body(c, carry):
        rows = pl.ds(c * CHUNK, CHUNK)
        hs = range(HEADS)
        lanes = [slice(h * HEAD_DIM, (h + 1) * HEAD_DIM) for h in hs]
        s_old = [s_ref[h] for h in hs]
        ws_qs = [_dot(jnp.concatenate([w_ref[rows, lanes[h]], qg_ref[rows, lanes[h]]], axis=0),
                      s_old[h].astype(BF16)) for h in hs]
        v_new = [(u_ref[rows, lanes[h]] - ws_qs[h][:CHUNK]).astype(BF16) for h in hs]
        for h in hs:
            s_ref[h] = (s_old[h] * dec_ref[c * HEADS + h][0:1, :]
                        + _dot_tn(kd_ref[rows, lanes[h]], v_new[h]))
        intra = []
        for pi in range(PAIRS):
            v_bd = jnp.concatenate([jnp.concatenate([v_new[2 * pi], zero_tile], axis=1),
                                    jnp.concatenate([zero_tile, v_new[2 * pi + 1]], axis=1)], axis=0)
            av = _dot(aqk_ref[c * PAIRS + pi], v_bd)
            intra += [av[:, :HEAD_DIM], av[:, HEAD_DIM:]]
        for h in hs:
            o = ws_qs[h][CHUNK:] + intra[h]
            o = o * lax.rsqrt(jnp.mean(o * o, axis=-1, keepdims=True) + EPS) * nw
            gate = gate_ref[rows, lanes[h]].astype(F32)
            o_ref[rows, lanes[h]] = (o * (gate * _sigmoid(gate))).astype(o_ref.dtype)
        return carry

    for c in range(tc // CHUNK):
        recur_body(c, 0)


def _gdn_selectors():
    src = np.arange(2 * AB_LANES)[:, None] % AB_LANES
    out = np.arange(HEADS * HEAD_DIM)[None, :]
    sel_beta = src == HEADS + out // HEAD_DIM
    out_g = out[:, :PAIRS * 2 * CHUNK]
    sel_g = src == 2 * (out_g // (2 * CHUNK)) + (out_g % (2 * CHUNK)) // CHUNK
    out_c = np.arange(3 * HEAD_DIM)[None, :]
    src_left = src < CHUNK
    sel_cum = np.where(out_c < HEAD_DIM, src_left == (out_c < CHUNK),
                       np.where(out_c < 2 * HEAD_DIM, src_left, ~src_left))
    ones = np.ones((HEAD_DIM, HEAD_DIM))
    return tuple(jnp.asarray(a, BF16) for a in (sel_beta, sel_g, sel_cum, ones))


def _gdn(h3, conv_w, ab3, abt, alog, dtb, norm_w, tc):
    b, t, _ = h3.shape
    nc = tc // CHUNK
    assert nc % PREP_CHUNKS == 0
    alog_row = jnp.zeros((1, AB_LANES), F32).at[0, :HEADS].set(alog)
    dtb_row = jnp.zeros((1, AB_LANES), F32).at[0, :HEADS].set(dtb)
    alog_pk = jnp.repeat(alog, CHUNK).reshape(PAIRS, 2 * CHUNK)
    dtb_pk = jnp.repeat(dtb, CHUNK).reshape(PAIRS, 2 * CHUNK)
    sel_beta, sel_g, sel_cum, ones = _gdn_selectors()
    kern = functools.partial(_gdn_kernel, tc=tc)
    return pl.pallas_call(
        kern,
        name="gdn",
        grid=(b, t // tc),
        in_specs=[
            pl.BlockSpec((None, tc, 3 * WIDTH), lambda bi, ti: (bi, ti, 0)),
            pl.BlockSpec((DN_CONV, 3 * WIDTH), lambda bi, ti: (0, 0)),
            pl.BlockSpec((None, tc, AB_LANES), lambda bi, ti: (bi, ti, 0)),
            pl.BlockSpec((None, nc, HEADS, 2 * CHUNK), lambda bi, ti: (bi, ti, 0, 0)),
            pl.BlockSpec((1, AB_LANES), lambda bi, ti: (0, 0)),
            pl.BlockSpec((1, AB_LANES), lambda bi, ti: (0, 0)),
            pl.BlockSpec((PAIRS, 2 * CHUNK), lambda bi, ti: (0, 0)),
            pl.BlockSpec((PAIRS, 2 * CHUNK), lambda bi, ti: (0, 0)),
            pl.BlockSpec(sel_beta.shape, lambda bi, ti: (0, 0)),
            pl.BlockSpec(sel_g.shape, lambda bi, ti: (0, 0)),
            pl.BlockSpec(sel_cum.shape, lambda bi, ti: (0, 0)),
            pl.BlockSpec(ones.shape, lambda bi, ti: (0, 0)),
            pl.BlockSpec((None, tc, WIDTH), lambda bi, ti: (bi, ti, COL_DN_GATE * 128 // WIDTH)),
            pl.BlockSpec((1, HEAD_DIM), lambda bi, ti: (0, 0)),
        ],
        out_specs=pl.BlockSpec((None, tc, WIDTH), lambda bi, ti: (bi, ti, 0)),
        out_shape=jax.ShapeDtypeStruct((b, t, WIDTH), BF16),
        scratch_shapes=[
            pltpu.VMEM((HEADS, HEAD_DIM, HEAD_DIM), F32),
            pltpu.VMEM((HALO, 3 * WIDTH), F32),
            pltpu.VMEM((HALO + tc, 3 * WIDTH), F32),
            pltpu.VMEM((tc, 3 * WIDTH), F32),
            pltpu.VMEM((tc, AB_LANES), F32),
            pltpu.VMEM((tc, AB_LANES), F32),
            pltpu.VMEM((nc, PAIRS, 2 * CHUNK), F32),
            pltpu.VMEM((tc, WIDTH), F32),
            pltpu.VMEM((tc, WIDTH), BF16),
            pltpu.VMEM((tc, WIDTH), BF16),
            pltpu.VMEM((tc, WIDTH), BF16),
            pltpu.VMEM((nc * PAIRS, CHUNK, 2 * CHUNK), BF16),
            pltpu.VMEM((nc * HEADS, HALO, HEAD_DIM), F32),
            pltpu.VMEM((nc * PAIRS, CHUNK, 2 * CHUNK), F32),
            pltpu.VMEM((nc * PAIRS, 2 * CHUNK, 2 * HEAD_DIM), BF16),
        ],
        compiler_params=pltpu.CompilerParams(
            dimension_semantics=("parallel", "arbitrary"),
            vmem_limit_bytes=VMEM_LIMIT),
    )(h3, conv_w, ab3, abt, alog_row, dtb_row, alog_pk, dtb_pk, sel_beta, sel_g, sel_cum, ones,
      h3, norm_w.reshape(1, HEAD_DIM))


SB_DEAD_LOG2 = -150.0
SB_Q_SCALE = HEAD_DIM ** -0.5 * math.log2(math.e)


def _sb_kernel(q_ref, k_ref, v_ref, o_ref, *, tq, nh):
    i = pl.program_id(2)
    row = lax.broadcasted_iota(jnp.int32, (tq, tq), 0)
    col = lax.broadcasted_iota(jnp.int32, (tq, tq), 1)
    later = jnp.where(row > col, 1.0, 0.0).astype(BF16)
    hs = range(nh)
    lanes = [slice(h * HEAD_DIM, (h + 1) * HEAD_DIM) for h in hs]
    qs = [q_ref[:, lanes[h]] for h in hs]

    def block(j, accs, runs, mask):
        k0 = pl.multiple_of(j * tq, tq)
        zs = [_dot_nt(qs[h], k_ref[pl.ds(k0, tq), lanes[h]]) for h in hs]
        keeps = []
        for z in zs:
            nz = -z
            keep = jnp.minimum(nz, 0.0) - jnp.log2(1.0 + jnp.exp2(jnp.minimum(z, nz)))
            keeps.append(keep if mask is None else jnp.where(mask, keep, 0.0))
        inner = [_dot(keep.astype(BF16), later) for keep in keeps]
        new_accs, new_runs = [], []
        for h in hs:
            a = jnp.exp2(zs[h] + keeps[h] + inner[h] + runs[h])
            if mask is not None:
                a = jnp.where(mask, a, 0.0)
            new_accs.append(accs[h] + _dot(a.astype(BF16), v_ref[pl.ds(k0, tq), lanes[h]]))
            new_runs.append(runs[h] + jnp.sum(keeps[h], axis=1, keepdims=True))
        return new_accs, new_runs

    accs = [jnp.zeros((tq, HEAD_DIM), F32) for _ in hs]
    runs = [jnp.zeros((tq, 1), F32) for _ in hs]
    accs, runs = block(i, accs, runs, col < row)

    def alive(runs):
        m = runs[0]
        for r in runs[1:]:
            m = jnp.maximum(m, r)
        return jnp.max(m) > SB_DEAD_LOG2

    def cond(carry):
        j, _, _, live = carry
        return jnp.logical_and(j >= 0, live)

    def body(carry):
        j, accs, runs, _ = carry
        accs, runs = block(j, list(accs), list(runs), None)
        return j - 1, tuple(accs), tuple(runs), alive(runs)

    _, accs, _, _ = lax.while_loop(cond, body, (i - 1, tuple(accs), tuple(runs), alive(runs)))
    for h in hs:
        o_ref[:, lanes[h]] = accs[h].astype(o_ref.dtype)


def _sb_attention(h3, tq, nh):
    b, t, _ = h3.shape
    kern = functools.partial(_sb_kernel, tq=tq, nh=nh)
    w = nh * HEAD_DIM
    return pl.pallas_call(
        kern,
        name="sb_attn",
        grid=(b, HEADS // nh, t // tq),
        in_specs=[
            pl.BlockSpec((None, tq, w), lambda bi, hi, qi: (bi, qi, COL_SB_Q // nh + hi)),
            pl.BlockSpec((None, t, w), lambda bi, hi, qi: (bi, 0, COL_SB_K // nh + hi)),
            pl.BlockSpec((None, t, w), lambda bi, hi, qi: (bi, 0, COL_SB_V // nh + hi)),
        ],
        out_specs=pl.BlockSpec((None, tq, w), lambda bi, hi, qi: (bi, qi, hi)),
        out_shape=jax.ShapeDtypeStruct((b, t, WIDTH), BF16),
        compiler_params=pltpu.CompilerParams(
            dimension_semantics=("parallel", "parallel", "arbitrary"),
            vmem_limit_bytes=VMEM_LIMIT),
    )(h3, h3, h3)


def _merge_kernel(x_ref, odn_ref, osb_ref, gdn_ref, gsb_ref, pa_ref, pb_ref, wo_ref, nw_ref,
                  x1_ref, n2_ref):
    pa = _dot(odn_ref[...], pa_ref[...])
    pb = _dot(osb_ref[...], pb_ref[...])
    mixed = (_sigmoid(gdn_ref[...].astype(F32)) * pa + _sigmoid(gsb_ref[...].astype(F32)) * pb)
    x1 = x_ref[...] + _dot(mixed.astype(BF16), wo_ref[...])
    x1_ref[...] = x1
    ms = jnp.mean(x1 * x1, axis=-1, keepdims=True)
    n2_ref[...] = (x1 * lax.rsqrt(ms + EPS) * nw_ref[...]).astype(n2_ref.dtype)


def _merge(x2, odn2, osb2, h2, p_dn, p_sb, w_out, norm_w, tm):
    m = x2.shape[0]
    gcol = COL_MERGE * 128 // D_MODEL
    row_spec = lambda c: pl.BlockSpec((tm, D_MODEL), lambda i: (i, c))
    w_spec = pl.BlockSpec((D_MODEL, D_MODEL), lambda i: (0, 0))
    return pl.pallas_call(
        _merge_kernel,
        name="merge",
        grid=(m // tm,),
        in_specs=[row_spec(0), row_spec(0), row_spec(0), row_spec(gcol), row_spec(gcol + 1),
                  w_spec, w_spec, w_spec, pl.BlockSpec((1, D_MODEL), lambda i: (0, 0))],
        out_specs=[row_spec(0), row_spec(0)],
        out_shape=[jax.ShapeDtypeStruct((m, D_MODEL), F32),
                   jax.ShapeDtypeStruct((m, D_MODEL), BF16)],
        compiler_params=pltpu.CompilerParams(
            dimension_semantics=("parallel",),
            vmem_limit_bytes=VMEM_LIMIT),
    )(x2, odn2, osb2, h2, h2, p_dn, p_sb, w_out, norm_w)


def _ffn_kernel(n2_ref, x1_ref, wup_ref, cw_ref, wd_ref, nw_ref, o_ref,
                ext_ref, carry_ref, *, tm, tf):
    ti = pl.program_id(1)
    n2 = n2_ref[...]

    def conv(cols, slot):
        u = _dot(n2, wup_ref[:, cols])
        ext = ext_ref.at[slot % 2]
        ext[0:HALO, :] = jnp.where(ti == 0, 0.0, carry_ref[slot])
        ext[HALO:HALO + tm, :] = u
        carry_ref[slot] = u[tm - HALO:, :]
        y = cw_ref[2:3, cols] * u
        y = y + cw_ref[1:2, cols] * ext[HALO - 1:HALO - 1 + tm, :]
        y = y + cw_ref[0:1, cols] * ext[HALO - 2:HALO - 2 + tm, :]
        return y

    x2 = x1_ref[...]
    for k in range(D_FF // tf):
        gate = conv(slice(k * tf, (k + 1) * tf), 2 * k)
        up = conv(slice(D_FF + k * tf, D_FF + (k + 1) * tf), 2 * k + 1)
        act = (gate * _sigmoid(gate) * up).astype(BF16)
        x2 = x2 + _dot(act, wd_ref[k * tf:(k + 1) * tf, :])
    ms = jnp.mean(x2 * x2, axis=-1, keepdims=True)
    o_ref[...] = x2 * lax.rsqrt(ms + EPS) * nw_ref[...]


def _ffn(n2, x1, w_up, conv_w, w_down, norm_w, tm, tf):
    b, t, _ = n2.shape
    nk = D_FF // tf
    kern = functools.partial(_ffn_kernel, tm=tm, tf=tf)
    resident = dict(pipeline_mode=pl.Buffered(1))
    return pl.pallas_call(
        kern,
        name="ffn",
        grid=(b, t // tm),
        in_specs=[
            pl.BlockSpec((None, tm, D_MODEL), lambda bi, ti: (bi, ti, 0)),
            pl.BlockSpec((None, tm, D_MODEL), lambda bi, ti: (bi, ti, 0)),
            pl.BlockSpec((D_MODEL, 2 * D_FF), lambda bi, ti: (0, 0), **resident),
            pl.BlockSpec((FFN_CONV, 2 * D_FF), lambda bi, ti: (0, 0), **resident),
            pl.BlockSpec((D_FF, D_MODEL), lambda bi, ti: (0, 0), **resident),
            pl.BlockSpec((1, D_MODEL), lambda bi, ti: (0, 0)),
        ],
        out_specs=pl.BlockSpec((None, tm, D_MODEL), lambda bi, ti: (bi, ti, 0)),
        out_shape=jax.ShapeDtypeStruct((b, t, D_MODEL), F32),
        scratch_shapes=[
            pltpu.VMEM((2, HALO + tm, tf), F32),
            pltpu.VMEM((2 * nk, HALO, tf), F32),
        ],
        compiler_params=pltpu.CompilerParams(
            dimension_semantics=("parallel", "arbitrary"),
            vmem_limit_bytes=VMEM_LIMIT),
    )(n2, x1, w_up, conv_w, w_down, norm_w)


def kernel(x, norm1_w, w_in, dn_conv_w, dn_A_log, dn_dt_bias, dn_norm_w, w_proj_dn, w_proj_sb,
           w_out, norm2_w, ffn_w_up, ffn_conv_w, ffn_w_down, norm_f_w):
    b, t, d = x.shape
    assert w_in.shape[0] == 1, "single-layer block: the final RMSNorm is fused into the FFN call"
    m = b * t
    tm = min(512, t)
    tc = min(256, t)
    tq = min(256, t)
    qkv_end = 3 * WIDTH
    b_end = qkv_end + 2 * HEADS
    g_end = b_end + WIDTH
    wl = w_in[0]
    col_scale = np.ones((1, wl.shape[1]), np.float32)
    col_scale[:, g_end:g_end + WIDTH] = SB_Q_SCALE
    w_bf = (wl * col_scale).astype(BF16)
    w_main = jnp.concatenate([w_bf[:, :qkv_end], w_bf[:, b_end:]], axis=1)
    w_ab = jnp.pad(w_bf[:, qkv_end:b_end], ((0, 0), (0, AB_LANES - 2 * HEADS)))

    h2, ab2 = _in_proj(x.reshape(m, d), norm1_w[0].reshape(1, d), w_main, w_ab, tm, H_COLS // 3)
    h3 = h2.reshape(b, t, H_COLS)
    ab3 = ab2.reshape(b, t, AB_LANES)
    abt = (ab3[:, :, :2 * HEADS].reshape(b, t // CHUNK, CHUNK, 2 * HEADS).transpose(0, 1, 3, 2)
           .reshape(b, t // CHUNK, HEADS, 2 * CHUNK))

    o_dn = _gdn(h3, dn_conv_w[0], ab3, abt, dn_A_log[0], dn_dt_bias[0], dn_norm_w[0], tc)
    o_sb = _sb_attention(h3, tq, 8)

    x1, n2 = _merge(x.reshape(m, d), o_dn.reshape(m, WIDTH), o_sb.reshape(m, WIDTH), h2,
                    w_proj_dn[0].astype(BF16), w_proj_sb[0].astype(BF16),
                    w_out[0].astype(BF16), norm2_w[0].reshape(1, d), min(512, t))

    return _ffn(n2.reshape(b, t, d), x1.reshape(b, t, d), ffn_w_up[0].astype(BF16),
                ffn_conv_w[0], ffn_w_down[0].astype(BF16), norm_f_w.reshape(1, d),
                min(512, t), D_FF // 2)
```

```python
import functools
import math

import numpy as np

import jax
import jax.numpy as jnp
from jax import lax
from jax.experimental import pallas as pl
from jax.experimental.pallas import tpu as pltpu

F32 = jnp.float32
BF16 = jnp.bfloat16

D_MODEL = 1024
HEADS = 8
HEAD_DIM = 128
WIDTH = HEADS * HEAD_DIM
DN_CONV = 4
CHUNK = 64
D_FF = 2816
FFN_CONV = 3
EPS = 1e-6

H_COLS = 3 * WIDTH + WIDTH + 3 * WIDTH + 2 * D_MODEL
COL_DN_GATE = 3 * WIDTH // 128
COL_SB_Q = COL_DN_GATE + WIDTH // 128
COL_SB_K = COL_SB_Q + HEADS
COL_SB_V = COL_SB_K + HEADS
COL_MERGE = COL_SB_V + HEADS
AB_LANES = 128
HALO = 8

VMEM_LIMIT = 56 * 1024 * 1024


def _dot(a, b):
    return jnp.dot(a, b, preferred_element_type=F32)


def _dot_nt(a, b):
    return lax.dot_general(a, b, (((1,), (1,)), ((), ())), preferred_element_type=F32)


def _dot_tn(a, b):
    return lax.dot_general(a, b, (((0,), (0,)), ((), ())), preferred_element_type=F32)


def _sigmoid(x):
    return 1.0 / (1.0 + jnp.exp(-x))


def _softplus(x):
    return jnp.maximum(x, 0.0) + jnp.log(1.0 + jnp.exp(-jnp.abs(x)))


def _w_in_prep_kernel(w_ref, wm_ref, wab_ref, *, ab_at, ab_cols, q_at):
    w = w_ref[...]
    wm_ref[:, :ab_at] = w[:, :ab_at].astype(BF16)
    rest = w[:, ab_at + ab_cols:]
    wm_ref[:, ab_at:ab_at + q_at] = rest[:, :q_at].astype(BF16)
    wm_ref[:, ab_at + q_at:ab_at + q_at + WIDTH] = (rest[:, q_at:q_at + WIDTH] * SB_Q_SCALE).astype(BF16)
    wm_ref[:, ab_at + q_at + WIDTH:] = rest[:, q_at + WIDTH:].astype(BF16)
    ab = w[:, ab_at:ab_at + ab_cols]
    pad = jnp.zeros((ab.shape[0], AB_LANES - ab_cols), F32)
    wab_ref[...] = jnp.concatenate([ab, pad], axis=1).astype(BF16)


def _w_in_prep(w, ab_at, b_end, q_at):
    k, n = w.shape
    ab_cols = b_end - ab_at
    rows = 128
    kern = functools.partial(_w_in_prep_kernel, ab_at=ab_at, ab_cols=ab_cols, q_at=q_at)
    return pl.pallas_call(
        kern,
        name="w_in_prep",
        grid=(k // rows,),
        in_specs=[pl.BlockSpec((rows, n), lambda i: (i, 0))],
        out_specs=[pl.BlockSpec((rows, n - ab_cols), lambda i: (i, 0)),
                   pl.BlockSpec((rows, AB_LANES), lambda i: (i, 0))],
        out_shape=[jax.ShapeDtypeStruct((k, n - ab_cols), BF16),
                   jax.ShapeDtypeStruct((k, AB_LANES), BF16)],
        compiler_params=pltpu.CompilerParams(
            dimension_semantics=("parallel",),
            vmem_limit_bytes=VMEM_LIMIT),
    )(w)


def _in_proj_kernel(x_ref, nw_ref, w_ref, wab_ref, h_ref, ab_ref, *, tn):
    x = x_ref[...]
    ms = jnp.mean(x * x, axis=-1, keepdims=True)
    xn = (x * lax.rsqrt(ms + EPS) * nw_ref[...]).astype(BF16)
    ab_ref[...] = _dot(xn, wab_ref[...])
    for j in range(h_ref.shape[1] // tn):
        cols = slice(j * tn, (j + 1) * tn)
        h_ref[:, cols] = _dot(xn, w_ref[:, cols]).astype(h_ref.dtype)


def _in_proj(x2, norm_w, w_main, w_ab, tm, tn):
    m = x2.shape[0]
    n = w_main.shape[1]
    kern = functools.partial(_in_proj_kernel, tn=tn)
    resident = dict(pipeline_mode=pl.Buffered(1))
    return pl.pallas_call(
        kern,
        name="in_proj",
        grid=(m // tm,),
        in_specs=[
            pl.BlockSpec((tm, D_MODEL), lambda i: (i, 0)),
            pl.BlockSpec((1, D_MODEL), lambda i: (0, 0)),
            pl.BlockSpec((D_MODEL, n), lambda i: (0, 0), **resident),
            pl.BlockSpec((D_MODEL, AB_LANES), lambda i: (0, 0), **resident),
        ],
        out_specs=[
            pl.BlockSpec((tm, n), lambda i: (i, 0)),
            pl.BlockSpec((tm, AB_LANES), lambda i: (i, 0)),
        ],
        out_shape=[
            jax.ShapeDtypeStruct((m, n), BF16),
            jax.ShapeDtypeStruct((m, AB_LANES), F32),
        ],
        compiler_params=pltpu.CompilerParams(
            dimension_semantics=("parallel",),
            vmem_limit_bytes=VMEM_LIMIT),
    )(x2, norm_w, w_main, w_ab)


PAIRS = HEADS // 2
PREP_CHUNKS = 4


def _hi_lo(a):
    hi = a.astype(BF16)
    return hi, (a - hi.astype(F32)).astype(BF16)


def _lhs3(hi, lo):
    return jnp.concatenate([hi, hi, lo], axis=1)


def _rhs3(hi, lo):
    return jnp.concatenate([hi, lo, hi], axis=0)


def _gdn_kernel(qkv_ref, cw_ref, ab_ref, abt_ref, alog_row_ref, dtb_row_ref,
                alog_pk_ref, dtb_pk_ref, sel_beta_ref, sel_g_ref, sel_cum_ref, ones_ref,
                gate_ref, nw_ref, o_ref,
                s_ref, halo_ref, ext_ref, act_ref, gcol_ref, bcol_ref, grow_ref,
                u_ref, w_ref, qg_ref, kd_ref, aqk_ref, dec_ref, l_ref, rhs_ref, *, tc):
    @pl.when(pl.program_id(1) == 0)
    def _():
        s_ref[...] = jnp.zeros_like(s_ref)
        halo_ref[...] = jnp.zeros_like(halo_ref)

    for j in range(3 * WIDTH // HEAD_DIM):
        ls = slice(j * HEAD_DIM, (j + 1) * HEAD_DIM)
        x = qkv_ref[:, ls].astype(F32)
        ext_ref[0:HALO, ls] = halo_ref[:, ls]
        ext_ref[HALO:HALO + tc, ls] = x
        halo_ref[:, ls] = x[tc - HALO:, :]
        y = cw_ref[DN_CONV - 1:DN_CONV, ls] * x
        for tap in range(DN_CONV - 1):
            shift = DN_CONV - 1 - tap
            y = y + cw_ref[tap:tap + 1, ls] * ext_ref[HALO - shift:HALO - shift + tc, ls]
        act_ref[:, ls] = y * _sigmoid(y)

    ab = ab_ref[...]
    gcol_ref[...] = -jnp.exp(alog_row_ref[...]) * _softplus(ab + dtb_row_ref[...])
    bcol_ref[...] = _sigmoid(ab)
    grow_ref[...] = -jnp.exp(alog_pk_ref[...]) * _softplus(abt_ref[:, :PAIRS, :] + dtb_pk_ref[...])

    row = lax.broadcasted_iota(jnp.int32, (CHUNK, 2 * CHUNK), 0)
    lane = lax.broadcasted_iota(jnp.int32, (CHUNK, 2 * CHUNK), 1)
    col = lane & (CHUNK - 1)
    left = lane < CHUNK
    lower = col <= row
    strict = col < row
    eye = jnp.where(row == col, 1.0, 0.0).astype(F32)
    zero_bf = jnp.zeros((CHUNK, 2 * CHUNK), BF16)
    zero_tile = jnp.zeros((CHUNK, HEAD_DIM), BF16)

    def same_block(log2_size):
        return (row >> log2_size) == (col >> log2_size)

    def block_diag(x):
        return jnp.concatenate([jnp.where(left, x, zero_bf), jnp.where(left, zero_bf, x)], axis=0)

    def chunk_inputs(c, carry):
        rows = pl.ds(c * CHUNK, CHUNK)
        beta_all = _dot(jnp.concatenate(_hi_lo(bcol_ref[rows, :]), axis=1), sel_beta_ref[...])
        g_pairs = _dot(jnp.concatenate(_hi_lo(gcol_ref[rows, :]), axis=1), sel_g_ref[...])
        g_rows = grow_ref[c]
        probs = [dict(c=c, rows=rows, pi=pi, beta_all=beta_all,
                      g_c=g_pairs[:, pi * 2 * CHUNK:(pi + 1) * 2 * CHUNK],
                      g_r=g_rows[pi:pi + 1, :]) for pi in range(PAIRS)]
        for p in probs:
            rows, pi = p["rows"], p["pi"]
            hs = (2 * pi, 2 * pi + 1)
            masked = jnp.where(lower, p["g_r"], 0.0)
            cum = _dot(jnp.concatenate(_hi_lo(masked), axis=1), sel_cum_ref[...])
            gc_c = cum[:, :2 * CHUNK]
            gfull = [cum[:, HEAD_DIM:2 * HEAD_DIM], cum[:, 2 * HEAD_DIM:]]
            gc_r = jnp.sum(jnp.where(row <= col, p["g_c"], 0.0), axis=0, keepdims=True)
            decay = jnp.where(lower, jnp.exp(jnp.where(lower, gc_c - gc_r, 0.0)), 0.0)
            qs, ks, kbs, rhs = [], [], [], []
            for side, h in enumerate(hs):
                lanes = slice(h * HEAD_DIM, (h + 1) * HEAD_DIM)
                q = act_ref[rows, h * HEAD_DIM:(h + 1) * HEAD_DIM]
                k = act_ref[rows, WIDTH + h * HEAD_DIM:WIDTH + (h + 1) * HEAD_DIM]
                v = act_ref[rows, 2 * WIDTH + h * HEAD_DIM:2 * WIDTH + (h + 1) * HEAD_DIM]
                sumsq = _dot(jnp.concatenate([q * q, k * k], axis=0).astype(BF16), ones_ref[...])
                q = q * (lax.rsqrt(sumsq[:CHUNK] + EPS) * (HEAD_DIM ** -0.5))
                k = k * lax.rsqrt(sumsq[CHUNK:] + EPS)
                beta = p["beta_all"][:, lanes]
                eg = jnp.exp(gfull[side])
                g_last = gfull[side][CHUNK - 1:CHUNK, :]
                kb = k * beta
                rhs.append(jnp.concatenate([v * beta, kb * eg], axis=1).astype(BF16))
                qs.append(q.astype(BF16))
                ks.append(k.astype(BF16))
                kbs.append(kb.astype(BF16))
                dec_ref[p["c"] * HEADS + h] = jnp.broadcast_to(jnp.exp(g_last), (HALO, HEAD_DIM))
                qg_ref[rows, lanes] = (q * eg).astype(BF16)
                kd_ref[rows, lanes] = (k * jnp.exp(g_last - gfull[side])).astype(BF16)
            rhs_ref[p["c"] * PAIRS + pi] = jnp.concatenate(rhs, axis=0)
            kq = jnp.concatenate([jnp.concatenate(kbs, axis=1), jnp.concatenate(qs, axis=1)], axis=0)
            kt = jnp.concatenate([jnp.concatenate([ks[0], zero_tile], axis=1),
                                  jnp.concatenate([zero_tile, ks[1]], axis=1)], axis=0)
            kk_qk = _dot_nt(kq, kt)
            l_ref[p["c"] * PAIRS + pi] = jnp.where(strict, kk_qk[:CHUNK] * decay, 0.0)
            aqk_ref[p["c"] * PAIRS + pi] = (kk_qk[CHUNK:] * decay).astype(BF16)
        return carry

    def solve(chunks):
        probs = [dict(c=c, pi=pi, rows=pl.ds(c * CHUNK, CHUNK), l=l_ref[c * PAIRS + pi])
                 for c in chunks for pi in range(PAIRS)]
        invs = [eye - jnp.where(same_block(1), p["l"], 0.0) for p in probs]
        for lvl in range(1, 6):
            off_diag = same_block(lvl + 1) & jnp.logical_not(same_block(lvl))
            cs = [jnp.where(off_diag, p["l"], 0.0).astype(BF16) for p in probs]
            ds = [d.astype(BF16) for d in invs]
            xs = [_dot(d, block_diag(c)).astype(BF16) for d, c in zip(ds, cs)]
            invs = [inv - _dot(x, block_diag(d)) for inv, x, d in zip(invs, xs, ds)]
        t0s = [_hi_lo(t) for t in invs]
        ls = [_hi_lo(p["l"]) for p in probs]
        resid = [(eye - t) - _dot(_lhs3(*l), _rhs3(block_diag(t0[0]), block_diag(t0[1])))
                 for t, t0, l in zip(invs, t0s, ls)]
        invs = [t + _dot(t0[0], block_diag(r.astype(BF16))) for t, t0, r in zip(invs, t0s, resid)]
        for p, inv in zip(probs, invs):
            hi, lo = _hi_lo(inv)
            rhs = rhs_ref[p["c"] * PAIRS + p["pi"]]
            rhs = jnp.concatenate([rhs, rhs], axis=0)
            for side in range(2):
                h = 2 * p["pi"] + side
                lanes = slice(h * HEAD_DIM, (h + 1) * HEAD_DIM)
                keep = left if side == 0 else jnp.logical_not(left)
                t_hi_lo = jnp.concatenate([jnp.where(keep, hi, zero_bf), jnp.where(keep, lo, zero_bf)], axis=1)
                uw = _dot(t_hi_lo, rhs)
                u_ref[p["rows"], lanes] = uw[:, :HEAD_DIM]
                w_ref[p["rows"], lanes] = uw[:, HEAD_DIM:].astype(BF16)

    for c in range(tc // CHUNK):
        chunk_inputs(c, 0)
    for i in range(tc // (PREP_CHUNKS * CHUNK)):
        solve([PREP_CHUNKS * i + j for j in range(PREP_CHUNKS)])

    nw = nw_ref[...]

    def recur_body(c, carry):
        rows = pl.ds(c * CHUNK, CHUNK)
        hs = range(HEADS)
        lanes = [slice(h * HEAD_DIM, (h + 1) * HEAD_DIM) for h in hs]
        s_old = [s_ref[h] for h in hs]
        ws_qs = [_dot(jnp.concatenate([w_ref[rows, lanes[h]], qg_ref[rows, lanes[h]]], axis=0),
                      s_old[h].astype(BF16)) for h in hs]
        v_new = [(u_ref[rows, lanes[h]] - ws_qs[h][:CHUNK]).astype(BF16) for h in hs]
        for h in hs:
            s_ref[h] = (s_old[h] * dec_ref[c * HEADS + h][0:1, :]
                        + _dot_tn(kd_ref[rows, lanes[h]], v_new[h]))
        intra = []
        for pi in range(PAIRS):
            v_bd = jnp.concatenate([jnp.concatenate([v_new[2 * pi], zero_tile], axis=1),
                                    jnp.concatenate([zero_tile, v_new[2 * pi + 1]], axis=1)], axis=0)
            av = _dot(aqk_ref[c * PAIRS + pi], v_bd)
            intra += [av[:, :HEAD_DIM], av[:, HEAD_DIM:]]
        for h in hs:
            o = ws_qs[h][CHUNK:] + intra[h]
            o = o * lax.rsqrt(jnp.mean(o * o, axis=-1, keepdims=True) + EPS) * nw
            gate = gate_ref[rows, lanes[h]].astype(F32)
            o_ref[rows, lanes[h]] = (o * (gate * _sigmoid(gate))).astype(o_ref.dtype)
        return carry

    for c in range(tc // CHUNK):
        recur_body(c, 0)


def _gdn_selectors():
    src = np.arange(2 * AB_LANES)[:, None] % AB_LANES
    out = np.arange(HEADS * HEAD_DIM)[None, :]
    sel_beta = src == HEADS + out // HEAD_DIM
    out_g = out[:, :PAIRS * 2 * CHUNK]
    sel_g = src == 2 * (out_g // (2 * CHUNK)) + (out_g % (2 * CHUNK)) // CHUNK
    out_c = np.arange(3 * HEAD_DIM)[None, :]
    src_left = src < CHUNK
    sel_cum = np.where(out_c < HEAD_DIM, src_left == (out_c < CHUNK),
                       np.where(out_c < 2 * HEAD_DIM, src_left, ~src_left))
    ones = np.ones((HEAD_DIM, HEAD_DIM))
    return tuple(jnp.asarray(a, BF16) for a in (sel_beta, sel_g, sel_cum, ones))


def _gdn(h3, conv_w, ab3, abt, alog, dtb, norm_w, tc):
    b, t, _ = h3.shape
    nc = tc // CHUNK
    assert nc % PREP_CHUNKS == 0
    alog_row = jnp.zeros((1, AB_LANES), F32).at[0, :HEADS].set(alog)
    dtb_row = jnp.zeros((1, AB_LANES), F32).at[0, :HEADS].set(dtb)
    alog_pk = jnp.repeat(alog, CHUNK).reshape(PAIRS, 2 * CHUNK)
    dtb_pk = jnp.repeat(dtb, CHUNK).reshape(PAIRS, 2 * CHUNK)
    sel_beta, sel_g, sel_cum, ones = _gdn_selectors()
    kern = functools.partial(_gdn_kernel, tc=tc)
    return pl.pallas_call(
        kern,
        name="gdn",
        grid=(b, t // tc),
        in_specs=[
            pl.BlockSpec((None, tc, 3 * WIDTH), lambda bi, ti: (bi, ti, 0)),
            pl.BlockSpec((DN_CONV, 3 * WIDTH), lambda bi, ti: (0, 0)),
            pl.BlockSpec((None, tc, AB_LANES), lambda bi, ti: (bi, ti, 0)),
            pl.BlockSpec((None, nc, HEADS, 2 * CHUNK), lambda bi, ti: (bi, ti, 0, 0)),
            pl.BlockSpec((1, AB_LANES), lambda bi, ti: (0, 0)),
            pl.BlockSpec((1, AB_LANES), lambda bi, ti: (0, 0)),
            pl.BlockSpec((PAIRS, 2 * CHUNK), lambda bi, ti: (0, 0)),
            pl.BlockSpec((PAIRS, 2 * CHUNK), lambda bi, ti: (0, 0)),
            pl.BlockSpec(sel_beta.shape, lambda bi, ti: (0, 0)),
            pl.BlockSpec(sel_g.shape, lambda bi, ti: (0, 0)),
            pl.BlockSpec(sel_cum.shape, lambda bi, ti: (0, 0)),
            pl.BlockSpec(ones.shape, lambda bi, ti: (0, 0)),
            pl.BlockSpec((None, tc, WIDTH), lambda bi, ti: (bi, ti, COL_DN_GATE * 128 // WIDTH)),
            pl.BlockSpec((1, HEAD_DIM), lambda bi, ti: (0, 0)),
        ],
        out_specs=pl.BlockSpec((None, tc, WIDTH), lambda bi, ti: (bi, ti, 0)),
        out_shape=jax.ShapeDtypeStruct((b, t, WIDTH), BF16),
        scratch_shapes=[
            pltpu.VMEM((HEADS, HEAD_DIM, HEAD_DIM), F32),
            pltpu.VMEM((HALO, 3 * WIDTH), F32),
            pltpu.VMEM((HALO + tc, 3 * WIDTH), F32),
            pltpu.VMEM((tc, 3 * WIDTH), F32),
            pltpu.VMEM((tc, AB_LANES), F32),
            pltpu.VMEM((tc, AB_LANES), F32),
            pltpu.VMEM((nc, PAIRS, 2 * CHUNK), F32),
            pltpu.VMEM((tc, WIDTH), F32),
            pltpu.VMEM((tc, WIDTH), BF16),
            pltpu.VMEM((tc, WIDTH), BF16),
            pltpu.VMEM((tc, WIDTH), BF16),
            pltpu.VMEM((nc * PAIRS, CHUNK, 2 * CHUNK), BF16),
            pltpu.VMEM((nc * HEADS, HALO, HEAD_DIM), F32),
            pltpu.VMEM((nc * PAIRS, CHUNK, 2 * CHUNK), F32),
            pltpu.VMEM((nc * PAIRS, 2 * CHUNK, 2 * HEAD_DIM), BF16),
        ],
        compiler_params=pltpu.CompilerParams(
            dimension_semantics=("parallel", "arbitrary"),
            vmem_limit_bytes=VMEM_LIMIT),
    )(h3, conv_w, ab3, abt, alog_row, dtb_row, alog_pk, dtb_pk, sel_beta, sel_g, sel_cum, ones,
      h3, norm_w.reshape(1, HEAD_DIM))


SB_DEAD_LOG2 = -150.0
SB_Q_SCALE = HEAD_DIM ** -0.5 * math.log2(math.e)


def _sb_kernel(q_ref, k_ref, v_ref, o_ref, *, tq, nh):
    i = pl.program_id(2)
    row = lax.broadcasted_iota(jnp.int32, (tq, tq), 0)
    col = lax.broadcasted_iota(jnp.int32, (tq, tq), 1)
    later = jnp.where(row > col, 1.0, 0.0).astype(BF16)
    hs = range(nh)
    lanes = [slice(h * HEAD_DIM, (h + 1) * HEAD_DIM) for h in hs]
    qs = [q_ref[:, lanes[h]] for h in hs]

    def block(j, accs, runs, mask):
        k0 = pl.multiple_of(j * tq, tq)
        zs = [_dot_nt(qs[h], k_ref[pl.ds(k0, tq), lanes[h]]) for h in hs]
        keeps = []
        for z in zs:
            nz = -z
            keep = jnp.minimum(nz, 0.0) - jnp.log2(1.0 + jnp.exp2(jnp.minimum(z, nz)))
            keeps.append(keep if mask is None else jnp.where(mask, keep, 0.0))
        inner = [_dot(keep.astype(BF16), later) for keep in keeps]
        new_accs, new_runs = [], []
        for h in hs:
            a = jnp.exp2(zs[h] + keeps[h] + inner[h] + runs[h])
            if mask is not None:
                a = jnp.where(mask, a, 0.0)
            new_accs.append(accs[h] + _dot(a.astype(BF16), v_ref[pl.ds(k0, tq), lanes[h]]))
            new_runs.append(runs[h] + jnp.sum(keeps[h], axis=1, keepdims=True))
        return new_accs, new_runs

    accs = [jnp.zeros((tq, HEAD_DIM), F32) for _ in hs]
    runs = [jnp.zeros((tq, 1), F32) for _ in hs]
    accs, runs = block(i, accs, runs, col < row)

    def alive(runs):
        m = runs[0]
        for r in runs[1:]:
            m = jnp.maximum(m, r)
        return jnp.max(m) > SB_DEAD_LOG2

    def cond(carry):
        j, _, _, live = carry
        return jnp.logical_and(j >= 0, live)

    def body(carry):
        j, accs, runs, _ = carry
        accs, runs = block(j, list(accs), list(runs), None)
        return j - 1, tuple(accs), tuple(runs), alive(runs)

    _, accs, _, _ = lax.while_loop(cond, body, (i - 1, tuple(accs), tuple(runs), alive(runs)))
    for h in hs:
        o_ref[:, lanes[h]] = accs[h].astype(o_ref.dtype)


def _sb_attention(h3, tq, nh):
    b, t, _ = h3.shape
    kern = functools.partial(_sb_kernel, tq=tq, nh=nh)
    w = nh * HEAD_DIM
    return pl.pallas_call(
        kern,
        name="sb_attn",
        grid=(b, HEADS // nh, t // tq),
        in_specs=[
            pl.BlockSpec((None, tq, w), lambda bi, hi, qi: (bi, qi, COL_SB_Q // nh + hi)),
            pl.BlockSpec((None, t, w), lambda bi, hi, qi: (bi, 0, COL_SB_K // nh + hi)),
            pl.BlockSpec((None, t, w), lambda bi, hi, qi: (bi, 0, COL_SB_V // nh + hi)),
        ],
        out_specs=pl.BlockSpec((None, tq, w), lambda bi, hi, qi: (bi, qi, hi)),
        out_shape=jax.ShapeDtypeStruct((b, t, WIDTH), BF16),
        compiler_params=pltpu.CompilerParams(
            dimension_semantics=("parallel", "parallel", "arbitrary"),
            vmem_limit_bytes=VMEM_LIMIT),
    )(h3, h3, h3)


def _merge_kernel(x_ref, odn_ref, osb_ref, gdn_ref, gsb_ref, pa_ref, pb_ref, wo_ref, nw_ref,
                  x1_ref, n2_ref):
    pa = _dot(odn_ref[...], pa_ref[...])
    pb = _dot(osb_ref[...], pb_ref[...])
    mixed = (_sigmoid(gdn_ref[...].astype(F32)) * pa + _sigmoid(gsb_ref[...].astype(F32)) * pb)
    x1 = x_ref[...] + _dot(mixed.astype(BF16), wo_ref[...])
    x1_ref[...] = x1
    ms = jnp.mean(x1 * x1, axis=-1, keepdims=True)
    n2_ref[...] = (x1 * lax.rsqrt(ms + EPS) * nw_ref[...]).astype(n2_ref.dtype)


def _merge(x2, odn2, osb2, h2, p_dn, p_sb, w_out, norm_w, tm):
    m = x2.shape[0]
    gcol = COL_MERGE * 128 // D_MODEL
    row_spec = lambda c: pl.BlockSpec((tm, D_MODEL), lambda i: (i, c))
    w_spec = pl.BlockSpec((D_MODEL, D_MODEL), lambda i: (0, 0))
    return pl.pallas_call(
        _merge_kernel,
        name="merge",
        grid=(m // tm,),
        in_specs=[row_spec(0), row_spec(0), row_spec(0), row_spec(gcol), row_spec(gcol + 1),
                  w_spec, w_spec, w_spec, pl.BlockSpec((1, D_MODEL), lambda i: (0, 0))],
        out_specs=[row_spec(0), row_spec(0)],
        out_shape=[jax.ShapeDtypeStruct((m, D_MODEL), F32),
                   jax.ShapeDtypeStruct((m, D_MODEL), BF16)],
        compiler_params=pltpu.CompilerParams(
            dimension_semantics=("parallel",),
            vmem_limit_bytes=VMEM_LIMIT),
    )(x2, odn2, osb2, h2, h2, p_dn, p_sb, w_out, norm_w)


def _ffn_kernel(n2_ref, x1_ref, wup_ref, cw_ref, wd_ref, nw_ref, o_ref,
                ext_ref, carry_ref, *, tm, tf):
    ti = pl.program_id(1)
    n2 = n2_ref[...]

    def conv(cols, slot):
        u = _dot(n2, wup_ref[:, cols])
        ext = ext_ref.at[slot % 2]
        ext[0:HALO, :] = jnp.where(ti == 0, 0.0, carry_ref[slot])
        ext[HALO:HALO + tm, :] = u
        carry_ref[slot] = u[tm - HALO:, :]
        y = cw_ref[2:3, cols] * u
        y = y + cw_ref[1:2, cols] * ext[HALO - 1:HALO - 1 + tm, :]
        y = y + cw_ref[0:1, cols] * ext[HALO - 2:HALO - 2 + tm, :]
        return y

    x2 = x1_ref[...]
    for k in range(D_FF // tf):
        gate = conv(slice(k * tf, (k + 1) * tf), 2 * k)
        up = conv(slice(D_FF + k * tf, D_FF + (k + 1) * tf), 2 * k + 1)
        act = (gate * _sigmoid(gate) * up).astype(BF16)
        x2 = x2 + _dot(act, wd_ref[k * tf:(k + 1) * tf, :])
    ms = jnp.mean(x2 * x2, axis=-1, keepdims=True)
    o_ref[...] = x2 * lax.rsqrt(ms + EPS) * nw_ref[...]


def _ffn(n2, x1, w_up, conv_w, w_down, norm_w, tm, tf):
    b, t, _ = n2.shape
    nk = D_FF // tf
    kern = functools.partial(_ffn_kernel, tm=tm, tf=tf)
    resident = dict(pipeline_mode=pl.Buffered(1))
    return pl.pallas_call(
        kern,
        name="ffn",
        grid=(b, t // tm),
        in_specs=[
            pl.BlockSpec((None, tm, D_MODEL), lambda bi, ti: (bi, ti, 0)),
            pl.BlockSpec((None, tm, D_MODEL), lambda bi, ti: (bi, ti, 0)),
            pl.BlockSpec((D_MODEL, 2 * D_FF), lambda bi, ti: (0, 0), **resident),
            pl.BlockSpec((FFN_CONV, 2 * D_FF), lambda bi, ti: (0, 0), **resident),
            pl.BlockSpec((D_FF, D_MODEL), lambda bi, ti: (0, 0), **resident),
            pl.BlockSpec((1, D_MODEL), lambda bi, ti: (0, 0)),
        ],
        out_specs=pl.BlockSpec((None, tm, D_MODEL), lambda bi, ti: (bi, ti, 0)),
        out_shape=jax.ShapeDtypeStruct((b, t, D_MODEL), F32),
        scratch_shapes=[
            pltpu.VMEM((2, HALO + tm, tf), F32),
            pltpu.VMEM((2 * nk, HALO, tf), F32),
        ],
        compiler_params=pltpu.CompilerParams(
            dimension_semantics=("parallel", "arbitrary"),
            vmem_limit_bytes=VMEM_LIMIT),
    )(n2, x1, w_up, conv_w, w_down, norm_w)


def kernel(x, norm1_w, w_in, dn_conv_w, dn_A_log, dn_dt_bias, dn_norm_w, w_proj_dn, w_proj_sb,
           w_out, norm2_w, ffn_w_up, ffn_conv_w, ffn_w_down, norm_f_w):
    b, t, d = x.shape
    assert w_in.shape[0] == 1, "single-layer block: the final RMSNorm is fused into the FFN call"
    m = b * t
    tm = min(512, t)
    tc = min(256, t)
    tq = min(256, t)
    qkv_end = 3 * WIDTH
    b_end = qkv_end + 2 * HEADS
    g_end = b_end + WIDTH
    w_main, w_ab = _w_in_prep(w_in.reshape(w_in.shape[1:]), qkv_end, b_end, g_end - b_end)

    h2, ab2 = _in_proj(x.reshape(m, d), norm1_w[0].reshape(1, d), w_main, w_ab, tm, H_COLS // 3)
    h3 = h2.reshape(b, t, H_COLS)
    ab3 = ab2.reshape(b, t, AB_LANES)
    abt = (ab3[:, :, :2 * HEADS].reshape(b, t // CHUNK, CHUNK, 2 * HEADS).transpose(0, 1, 3, 2)
           .reshape(b, t // CHUNK, HEADS, 2 * CHUNK))

    o_dn = _gdn(h3, dn_conv_w[0], ab3, abt, dn_A_log[0], dn_dt_bias[0], dn_norm_w[0], tc)
    o_sb = _sb_attention(h3, tq, 8)

    x1, n2 = _merge(x.reshape(m, d), o_dn.reshape(m, WIDTH), o_sb.reshape(m, WIDTH), h2,
                    w_proj_dn[0].astype(BF16), w_proj_sb[0].astype(BF16),
                    w_out[0].astype(BF16), norm2_w[0].reshape(1, d), min(512, t))

    return _ffn(n2.reshape(b, t, d), x1.reshape(b, t, d), ffn_w_up[0].astype(BF16),
                ffn_conv_w[0], ffn_w_down[0].astype(BF16), norm_f_w.reshape(1, d),
                min(512, t), D_FF // 2)
```
